```python
import jax, jax.numpy as jnp
from jax import lax
import numpy as np

D_MODEL = 4096
BATCH = 4
SEQ = 2048
DEPTH = 1
DEC_BATCH = 128
DEC_SEQ = 4
PAST_LEN = 16384
PAGE_SIZE = 128

N_MOD = 6
SSD_EXPAND = 2
D_INNER = SSD_EXPAND * D_MODEL
SSD_HEADDIM = 64
SSD_HEADS = D_INNER // SSD_HEADDIM
SSD_GROUPS = 8
SSD_HPG = SSD_HEADS // SSD_GROUPS
D_STATE = 128
CONV_W = 4
CONV_DIM = D_INNER + 2 * SSD_GROUPS * D_STATE
SSD_CHUNK = 128
GMLP_CHUNK = 128
GMLP_GROUPS = 8
D_GMLP = D_MODEL
GMLP_GW = D_GMLP // GMLP_GROUPS
PEER_HEADS = 8
PEER_NKEYS = 128
PEER_EXPERTS = PEER_NKEYS * PEER_NKEYS
PEER_DQ = 256
PEER_HALF = PEER_DQ // 2
PEER_TOPK = 16
PEER_BLOCK = 128
OFF_XBC = D_INNER
OFF_DT = OFF_XBC + CONV_DIM
OFF_U = OFF_DT + SSD_HEADS
OFF_V = OFF_U + D_GMLP
OFF_GA = OFF_V + D_GMLP
OFF_GB = OFF_GA + D_MODEL
IN_COLS = OFF_GB + D_MODEL
EPS = 1e-6

kernel_name = 'hybrid_ssd_gmlp_peer_step'


def rmsnorm(x, g):
    xf = x.astype(jnp.float32)
    y = xf * lax.rsqrt(jnp.mean(xf * xf, axis=-1, keepdims=True) + EPS)
    return (y * g.astype(jnp.float32)).astype(x.dtype)


def group_rmsnorm(y, g, groups):
    shp = y.shape
    yg = y.reshape(shp[:-1] + (groups, shp[-1] // groups))
    yg = yg * lax.rsqrt(jnp.mean(yg * yg, axis=-1, keepdims=True) + EPS)
    return yg.reshape(shp) * g.astype(jnp.float32)


def causal_dwconv(xbc, conv0, w, b):
    L = xbc.shape[1]
    xp = jnp.concatenate([conv0.astype(xbc.dtype), xbc], axis=1)
    y = b + sum(xp[:, k:k + L] * w[k] for k in range(CONV_W))
    return y, xp[:, L:]


def ssd_scan(x, dt, a, bm, cm, h0):
    f32 = jnp.float32
    b, L = x.shape[0], x.shape[1]
    cl = min(SSD_CHUNK, L)
    nc = -(-L // cl)
    pad = nc * cl - L
    x, dt, bm, cm = [jnp.pad(t.astype(f32), [(0, 0), (0, pad)] + [(0, 0)] * (t.ndim - 2)) for t in (x, dt, bm, cm)]
    xr = x.reshape(b, nc, cl, SSD_GROUPS, SSD_HPG, SSD_HEADDIM)
    dtr = dt.reshape(b, nc, cl, SSD_GROUPS, SSD_HPG)
    br = bm.reshape(b, nc, cl, SSD_GROUPS, D_STATE)
    cr = cm.reshape(b, nc, cl, SSD_GROUPS, D_STATE)
    acum = jnp.cumsum(dtr * a.reshape(SSD_GROUPS, SSD_HPG), axis=2)
    xdt = xr * dtr[..., None]
    at = jnp.moveaxis(acum, 2, -1)
    causal = jnp.tril(jnp.ones((cl, cl), bool))
    decay = jnp.exp(jnp.where(causal, at[..., :, None] - at[..., None, :], -jnp.inf))
    cb = jnp.einsum('bclgn,bcsgn->bcgls', cr, br)
    y_diag = jnp.einsum('bcgls,bcgrls,bcsgrp->bclgrp', cb, decay, xdt)
    tail = jnp.exp(acum[:, :, -1:] - acum)
    states = jnp.einsum('bclgn,bclgr,bclgrp->bcgrpn', br, tail, xdt)
    chunk_decay = jnp.exp(acum[:, :, -1])

    def step(h, inp):
        st, dec = inp
        return h * dec[..., None, None] + st, h

    h_init = h0.astype(f32).reshape(b, SSD_GROUPS, SSD_HPG, SSD_HEADDIM, D_STATE)
    h_last, h_in = lax.scan(step, h_init, (jnp.moveaxis(states, 1, 0), jnp.moveaxis(chunk_decay, 1, 0)))
    h_in = jnp.moveaxis(h_in, 0, 1)
    y_off = jnp.einsum('bclgn,bcgrpn,bclgr->bclgrp', cr, h_in, jnp.exp(acum))
    y = (y_diag + y_off).reshape(b, nc * cl, SSD_HEADS, SSD_HEADDIM)[:, :L]
    return y, h_last.reshape(b, SSD_HEADS, SSD_HEADDIM, D_STATE)


def chunk_spatial_gating(u, v, ln_g, ln_b, ws, bs):
    vf = v.astype(jnp.float32)
    mu = jnp.mean(vf, axis=-1, keepdims=True)
    var = jnp.mean(jnp.square(vf - mu), axis=-1, keepdims=True)
    vn = ((vf - mu) * lax.rsqrt(var + EPS) * ln_g + ln_b).astype(v.dtype)
    b, L, _ = v.shape
    cl = min(GMLP_CHUNK, L)
    nc = -(-L // cl)
    pad = nc * cl - L
    vr = jnp.pad(vn, ((0, 0), (0, pad), (0, 0))).reshape(b, nc, cl, GMLP_GROUPS, GMLP_GW)
    w = jnp.where(jnp.tril(jnp.ones((cl, cl), bool)), ws[:, :cl, :cl], 0.0)
    s = jnp.einsum('gts,bcsgk->bctgk', w, vr) + jnp.transpose(bs[:, :cl])[:, :, None]
    s = s.reshape(b, nc * cl, D_GMLP)[:, :L]
    return u * s, vn


def peer_ffn(h, wq, keys, u_tab, v_tab):
    b, L, d = h.shape
    T = b * L
    nb = -(-T // PEER_BLOCK)
    t = jnp.pad(h.reshape(T, d), ((0, nb * PEER_BLOCK - T), (0, 0)))

    def block(tb):
        q = (tb @ wq).reshape(-1, PEER_HEADS, 2, PEER_HALF)
        s = jnp.einsum('thik,hink->thin', q, keys).astype(jnp.float32)
        sv, si = lax.top_k(s, PEER_TOPK)
        cand = (sv[:, :, 0, :, None] + sv[:, :, 1, None, :]).reshape(-1, PEER_HEADS, PEER_TOPK * PEER_TOPK)
        cidx = (si[:, :, 0, :, None] * PEER_NKEYS + si[:, :, 1, None, :]).reshape(-1, PEER_HEADS, PEER_TOPK * PEER_TOPK)
        top, pos = lax.top_k(cand, PEER_TOPK)
        eidx = jnp.take_along_axis(cidx, pos, axis=-1)
        g = jax.nn.softmax(top, axis=-1).astype(tb.dtype)
        act = jax.nn.gelu(jnp.einsum('td,thkd->thk', tb, u_tab[eidx]), approximate=False)
        return jnp.einsum('thk,thkd->td', g * act, v_tab[eidx])

    out = lax.map(block, t.reshape(nb, PEER_BLOCK, d))
    return out.reshape(nb * PEER_BLOCK, d)[:T].reshape(b, L, d)


def trunk_layer(x, c, ssm0, conv0, w_ada, b_ada, norm1_g, w_in, conv_w, conv_b, dt_bias, a_log, d_skip,
                ssd_norm_g, w_oa, gmlp_ln_g, gmlp_ln_b, gmlp_ws, gmlp_bs, w_ob, w_out, norm2_g,
                peer_wq, peer_keys, peer_u, peer_v):
    f32 = jnp.float32
    b, L, _ = x.shape
    mod = jax.nn.silu(c) @ w_ada + b_ada
    shift1, scale1, gate1, shift2, scale2, gate2 = jnp.split(mod[:, None, :], N_MOD, axis=-1)
    h = rmsnorm(x, norm1_g) * (1 + scale1) + shift1
    proj = h @ w_in
    z, xbc, dt, u, v, ga, gb = jnp.split(proj, [OFF_XBC, OFF_DT, OFF_U, OFF_V, OFF_GA, OFF_GB], axis=-1)
    xbc, conv_new = causal_dwconv(xbc, conv0, conv_w, conv_b)
    xbc = jax.nn.silu(xbc)
    xs, bm, cm = jnp.split(xbc, [D_INNER, D_INNER + SSD_GROUPS * D_STATE], axis=-1)
    xs = xs.reshape(b, L, SSD_HEADS, SSD_HEADDIM)
    bm = bm.reshape(b, L, SSD_GROUPS, D_STATE)
    cm = cm.reshape(b, L, SSD_GROUPS, D_STATE)
    dt = jax.nn.softplus(dt.astype(f32) + dt_bias.astype(f32))
    a = -jnp.exp(a_log.astype(f32))
    y, h_new = ssd_scan(xs, dt, a, bm, cm, ssm0)
    y = (y + d_skip.astype(f32)[:, None] * xs.astype(f32)).reshape(b, L, D_INNER)
    y = group_rmsnorm(y * jax.nn.silu(z.astype(f32)), ssd_norm_g, SSD_GROUPS).astype(x.dtype)
    y_a = y @ w_oa
    y_b, v_rows = chunk_spatial_gating(jax.nn.gelu(u, approximate=False), jax.nn.gelu(v, approximate=False),
                                       gmlp_ln_g, gmlp_ln_b, gmlp_ws, gmlp_bs)
    y_b = y_b @ w_ob
    merged = jax.nn.sigmoid(ga) * y_a + jax.nn.sigmoid(gb) * y_b
    x = x + gate1 * (merged @ w_out)
    h2 = rmsnorm(x, norm2_g) * (1 + scale2) + shift2
    x = x + gate2 * peer_ffn(h2, peer_wq, peer_keys, peer_u, peer_v)
    return x, h_new, conv_new, v_rows


def setup_inputs(seed: int = 0) -> dict:
    key = jax.random.key(seed)
    ks = jax.random.split(key, 32)
    f32 = jnp.float32

    def nrm(k, shape, s):
        return jax.random.normal(k, shape, f32) * s

    dt0 = jnp.exp(jax.random.uniform(ks[12], (DEPTH, SSD_HEADS), f32, np.log(1e-3), np.log(1e-1)))
    return {
        'x_prompt': nrm(ks[0], (BATCH, SEQ, D_MODEL), 1.0),
        'x_sample': nrm(ks[1], (DEC_BATCH, DEC_SEQ, D_MODEL), 1.0),
        'state_ssm': nrm(ks[2], (DEPTH, DEC_BATCH, SSD_HEADS, SSD_HEADDIM, D_STATE), 0.1),
        'state_conv': nrm(ks[3], (DEPTH, DEC_BATCH, CONV_W - 1, CONV_DIM), 1.0),
        'c_prompt': nrm(ks[4], (BATCH, D_MODEL), 1.0),
        'c_sample': nrm(ks[5], (DEC_BATCH, D_MODEL), 1.0),
        'w_ada': nrm(ks[6], (DEPTH, D_MODEL, N_MOD * D_MODEL), 0.5 * D_MODEL ** -0.5),
        'b_ada': nrm(ks[7], (DEPTH, N_MOD * D_MODEL), 0.02),
        'norm1_g': 1.0 + nrm(ks[8], (DEPTH, D_MODEL), 0.02),
        'w_in': nrm(ks[9], (DEPTH, D_MODEL, IN_COLS), D_MODEL ** -0.5),
        'conv_w': nrm(ks[10], (DEPTH, CONV_W, CONV_DIM), CONV_W ** -0.5),
        'conv_b': nrm(ks[11], (DEPTH, CONV_DIM), 0.02),
        'dt_bias': dt0 + jnp.log(-jnp.expm1(-dt0)),
        'a_log': jnp.log(jax.random.uniform(ks[13], (DEPTH, SSD_HEADS), f32, 1.0, 16.0)),
        'd_skip': 1.0 + nrm(ks[14], (DEPTH, SSD_HEADS), 0.02),
        'ssd_norm_g': 1.0 + nrm(ks[15], (DEPTH, D_INNER), 0.02),
        'w_oa': nrm(ks[16], (DEPTH, D_INNER, D_MODEL), D_INNER ** -0.5),
        'gmlp_ln_g': 1.0 + nrm(ks[17], (DEPTH, D_GMLP), 0.02),
        'gmlp_ln_b': nrm(ks[18], (DEPTH, D_GMLP), 0.02),
        'gmlp_ws': nrm(ks[19], (DEPTH, GMLP_GROUPS, GMLP_CHUNK, GMLP_CHUNK), GMLP_CHUNK ** -0.5),
        'gmlp_bs': 1.0 + nrm(ks[20], (DEPTH, GMLP_GROUPS, GMLP_CHUNK), 0.02),
        'w_ob': nrm(ks[21], (DEPTH, D_GMLP, D_MODEL), D_GMLP ** -0.5),
        'w_out': nrm(ks[22], (DEPTH, D_MODEL, D_MODEL), D_MODEL ** -0.5),
        'norm2_g': 1.0 + nrm(ks[23], (DEPTH, D_MODEL), 0.02),
        'peer_wq': nrm(ks[24], (DEPTH, D_MODEL, PEER_HEADS * PEER_DQ), D_MODEL ** -0.5),
        'peer_keys': nrm(ks[25], (DEPTH, PEER_HEADS, 2, PEER_NKEYS, PEER_HALF), PEER_HALF ** -0.5),
        'peer_u': nrm(ks[26], (DEPTH, PEER_EXPERTS, D_MODEL), D_MODEL ** -0.5),
        'peer_v': nrm(ks[27], (DEPTH, PEER_EXPERTS, D_MODEL), PEER_HEADS ** -0.5),
        'final_g': 1.0 + nrm(ks[28], (D_MODEL,), 0.02),
    }


def reference(x_prompt, x_sample, state_ssm, state_conv, c_prompt, c_sample, w_ada, b_ada, norm1_g, w_in,
              conv_w, conv_b, dt_bias, a_log, d_skip, ssd_norm_g, w_oa, gmlp_ln_g, gmlp_ln_b, gmlp_ws, gmlp_bs,
              w_ob, w_out, norm2_g, peer_wq, peer_keys, peer_u, peer_v, final_g):
    xp, xs = x_prompt, x_sample
    ssm_p, conv_p, ssm_s, conv_s, v_s = [], [], [], [], []
    for l in range(DEPTH):
        lw = (w_ada[l], b_ada[l], norm1_g[l], w_in[l], conv_w[l], conv_b[l], dt_bias[l], a_log[l], d_skip[l],
              ssd_norm_g[l], w_oa[l], gmlp_ln_g[l], gmlp_ln_b[l], gmlp_ws[l], gmlp_bs[l], w_ob[l], w_out[l],
              norm2_g[l], peer_wq[l], peer_keys[l], peer_u[l], peer_v[l])
        ssm0 = jnp.zeros((xp.shape[0], SSD_HEADS, SSD_HEADDIM, D_STATE), jnp.float32)
        conv0 = jnp.zeros((xp.shape[0], CONV_W - 1, CONV_DIM), xp.dtype)
        xp, hp, cp, _ = trunk_layer(xp, c_prompt, ssm0, conv0, *lw)
        xs, hs, cs, vs = trunk_layer(xs, c_sample, state_ssm[l], state_conv[l], *lw)
        ssm_p.append(hp)
        conv_p.append(cp)
        ssm_s.append(hs)
        conv_s.append(cs)
        v_s.append(vs)
    y_prompt = rmsnorm(xp, final_g)
    y_sample = rmsnorm(xs, final_g)
    new_ssm_prompt = jnp.stack(ssm_p).astype(state_ssm.dtype)
    new_conv_prompt = jnp.stack(conv_p).astype(state_conv.dtype)
    new_ssm_sample = jnp.stack(ssm_s).astype(state_ssm.dtype)
    new_conv_sample = jnp.stack(conv_s).astype(state_conv.dtype)
    new_gmlp_v_sample = jnp.stack(v_s)
    return (y_prompt, y_sample, new_ssm_prompt, new_conv_prompt, new_ssm_sample, new_conv_sample, new_gmlp_v_sample)
```

```python
import functools

import jax
import jax.numpy as jnp
from jax import lax
from jax.experimental import pallas as pl
from jax.experimental.pallas import tpu as pltpu

F32 = jnp.float32
BF16 = jnp.bfloat16
EPS = 1e-6
N_MOD = 6
PEER_TOPK = 16
SSD_CHUNK = 128
GMLP_CHUNK = 128
SUBLANES = 8
LANES = 128
VMEM_LIMIT = 56 * 1024 * 1024
MATMUL_VMEM_BUDGET = 40 * 1024 * 1024
_SQRT_HALF = 0.7071067811865476
_HIGHEST = lax.Precision.HIGHEST
_NT = (((1,), (1,)), ((), ()))
_TN = (((0,), (0,)), ((), ()))


def _cparams(*sem):
    return pltpu.CompilerParams(dimension_semantics=sem, vmem_limit_bytes=VMEM_LIMIT)


def _silu(x):
    return x * jax.nn.sigmoid(x)


def _gelu(x):
    return 0.5 * x * (1.0 + lax.erf(x * _SQRT_HALF))


def _softplus(x):
    return jnp.maximum(x, 0.0) + jnp.log1p(jnp.exp(-jnp.abs(x)))


def _mean_sq_norm(x):
    return x * lax.rsqrt(jnp.mean(x * x, axis=-1, keepdims=True) + EPS)


def _ada_body(c_ref, w_ref, b_ref, o_ref):
    a = _silu(c_ref[...]).astype(BF16)
    o_ref[...] = jnp.dot(a, w_ref[...].astype(BF16), preferred_element_type=F32) + b_ref[...]


def _ada(c_all, w_ada, b_ada):
    m, d = c_all.shape
    n = w_ada.shape[1]
    tn = min(512, n)
    return pl.pallas_call(
        _ada_body,
        grid=(n // tn,),
        in_specs=[pl.BlockSpec((m, d), lambda j: (0, 0)),
                  pl.BlockSpec((d, tn), lambda j: (0, j)),
                  pl.BlockSpec((1, tn), lambda j: (0, j))],
        out_specs=pl.BlockSpec((m, tn), lambda j: (0, j)),
        out_shape=jax.ShapeDtypeStruct((m, n), F32),
        compiler_params=_cparams("arbitrary"),
        name="ada",
    )(c_all, w_ada, b_ada)


def _mod_spec(mod, k, tm, tn, grid_rank):
    per_seq = mod.shape[1] == 1
    nblk = mod.shape[2] // N_MOD // tn

    if grid_rank == 2:
        if per_seq:
            return pl.BlockSpec((1, 1, tn), lambda g, i: (g, 0, k))
        return pl.BlockSpec((1, tm, tn), lambda g, i: (0, i, k))
    if per_seq:
        return pl.BlockSpec((1, 1, tn), lambda j, g, i: (g, 0, k * nblk + j))
    return pl.BlockSpec((1, tm, tn), lambda j, g, i: (0, i, k * nblk + j))


def _row_tile(r, target):
    t = min(r, target)
    while r % t:
        t //= 2
    return t


def _norm_mod_body(x_ref, g_ref, sc_ref, sh_ref, o_ref):
    y = _mean_sq_norm(x_ref[0]) * g_ref[...]
    o_ref[0] = (y * (1.0 + sc_ref[0]) + sh_ref[0]).astype(o_ref.dtype)


def _norm_mod(x3, g, mod, k_shift, k_scale):
    gg, r, d = x3.shape
    tm = _row_tile(r, 512)
    return pl.pallas_call(
        _norm_mod_body,
        grid=(gg, r // tm),
        in_specs=[pl.BlockSpec((1, tm, d), lambda b, i: (b, i, 0)),
                  pl.BlockSpec((1, d), lambda b, i: (0, 0)),
                  _mod_spec(mod, k_scale, tm, d, 2),
                  _mod_spec(mod, k_shift, tm, d, 2)],
        out_specs=pl.BlockSpec((1, tm, d), lambda b, i: (b, i, 0)),
        out_shape=jax.ShapeDtypeStruct((gg, r, d), BF16),
        compiler_params=_cparams("arbitrary", "arbitrary"),
        name="norm_mod",
    )(x3, g, mod, mod)


def _mm_body(*refs, epilogue, kinds):
    a_ref, w_ref = refs[0], refs[1]
    o_ref = refs[-1]
    acc = jnp.dot(a_ref[0], w_ref[...], preferred_element_type=F32)
    ex = [r[...] if kd == "bias" else r[0] for r, kd in zip(refs[2:-1], kinds)]
    o_ref[0] = epilogue(acc, *ex).astype(o_ref.dtype)


def _matmul(a3, w, epilogue, extras=(), out_dtype=F32, tm_target=1024, tn_target=1024, name="mm"):
    gg, r, k = a3.shape
    n = w.shape[1]
    tm = _row_tile(r, tm_target)
    tn = _row_tile(n, tn_target)
    tile_bytes = sum(e[0].dtype.itemsize for e in extras if e[1] == "tile") + jnp.dtype(out_dtype).itemsize

    def vmem_bytes(tm_, tn_):
        return 2 * (tm_ * k * 2 + k * tn_ * 2 + tm_ * tn_ * tile_bytes) + 2 * tm_ * tn_ * 4

    while vmem_bytes(tm, tn) > MATMUL_VMEM_BUDGET and (tm > 256 or tn > 256):
        if tm >= tn and tm > 256:
            tm //= 2
        else:
            tn //= 2
    in_specs = [pl.BlockSpec((1, tm, k), lambda j, g, i: (g, i, 0)),
                pl.BlockSpec((k, tn), lambda j, g, i: (0, j))]
    args = [a3, w]
    kinds = []
    for e in extras:
        arr, kind = e[0], e[1]
        kinds.append(kind)
        args.append(arr)
        if kind == "tile":
            off = e[2] if len(e) > 2 else 0
            nb = n // tn
            in_specs.append(pl.BlockSpec((1, tm, tn), lambda j, g, i, off=off, nb=nb: (g, i, off * nb + j)))
        elif kind == "bias":
            in_specs.append(pl.BlockSpec((1, tn), lambda j, g, i: (0, j)))
        else:
            in_specs.append(_mod_spec(arr, e[2], tm, tn, 3))
    return pl.pallas_call(
        functools.partial(_mm_body, epilogue=epilogue, kinds=tuple(kinds)),
        grid=(n // tn, gg, r // tm),
        in_specs=in_specs,
        out_specs=pl.BlockSpec((1, tm, tn), lambda j, g, i: (g, i, j)),
        out_shape=jax.ShapeDtypeStruct((gg, r, n), out_dtype),
        compiler_params=_cparams("arbitrary", "arbitrary", "arbitrary"),
        name=name,
    )(*args)


def _ep_none(acc):
    return acc


def _ep_gelu(acc):
    return _gelu(acc)


def _ep_sigmoid(acc):
    return jax.nn.sigmoid(acc)


def _ep_softplus_bias(acc, bias):
    return _softplus(acc + bias)


def _ep_merge(acc, y_a, sg_a, sg_b):
    return sg_a * y_a + sg_b * acc


def _ep_residual(acc, x, gate):
    return x + gate * acc


def _conv_prompt_body(x_ref, w_ref, b_ref, o_ref, pad_ref, *, width):
    ln = x_ref.shape[1]
    pad_ref[0:SUBLANES, :] = jnp.zeros((SUBLANES, pad_ref.shape[1]), F32)
    pad_ref[SUBLANES:SUBLANES + ln, :] = x_ref[0]
    acc = b_ref[...] + x_ref[0] * w_ref[width - 1:width, :]
    for k in range(width - 1):
        s = SUBLANES - (width - 1) + k
        acc = acc + pad_ref[s:s + ln, :] * w_ref[k:k + 1, :]
    o_ref[0] = _silu(acc)


def _conv_prompt(x3, conv_w, conv_b):
    b, ln, c = x3.shape
    width = conv_w.shape[0]
    ct = _row_tile(c, 512)
    return pl.pallas_call(
        functools.partial(_conv_prompt_body, width=width),
        grid=(b, c // ct),
        in_specs=[pl.BlockSpec((1, ln, ct), lambda i, j: (i, 0, j)),
                  pl.BlockSpec((width, ct), lambda i, j: (0, j)),
                  pl.BlockSpec((1, ct), lambda i, j: (0, j))],
        out_specs=pl.BlockSpec((1, ln, ct), lambda i, j: (i, 0, j)),
        out_shape=jax.ShapeDtypeStruct((b, ln, c), F32),
        scratch_shapes=[pltpu.VMEM((ln + SUBLANES, ct), F32)],
        compiler_params=_cparams("arbitrary", "arbitrary"),
        name="conv_prompt",
    )(x3, conv_w, conv_b)


def _conv_slab_body(xp_ref, w_ref, b_ref, o_ref, *, width):
    for t in range(o_ref.shape[0]):
        acc = b_ref[...] + xp_ref[t] * w_ref[0:1, :]
        for k in range(1, width):
            acc = acc + xp_ref[t + k] * w_ref[k:k + 1, :]
        o_ref[t] = _silu(acc)


def _conv_slab(xp, conv_w, conv_b):
    lp, b, c = xp.shape
    width = conv_w.shape[0]
    ln = lp - width + 1
    ct = _row_tile(c, 1024)
    return pl.pallas_call(
        functools.partial(_conv_slab_body, width=width),
        grid=(c // ct,),
        in_specs=[pl.BlockSpec((lp, b, ct), lambda j: (0, 0, j)),
                  pl.BlockSpec((width, ct), lambda j: (0, j)),
                  pl.BlockSpec((1, ct), lambda j: (0, j))],
        out_specs=pl.BlockSpec((ln, b, ct), lambda j: (0, 0, j)),
        out_shape=jax.ShapeDtypeStruct((ln, b, c), F32),
        compiler_params=_cparams("arbitrary"),
        name="conv_sample",
    )(xp, conv_w, conv_b)


def _ssd_prompt_body(x_ref, b_ref, c_ref, dt_ref, z_ref, alog_ref, dsk_ref, ng_ref, y_ref, hl_ref,
                     h_scr, y_scr, *, headdim):
    ci = pl.program_id(2)

    @pl.when(ci == 0)
    def _():
        h_scr[...] = jnp.zeros_like(h_scr)

    x = x_ref[0]
    cl, gw = x.shape
    dt = dt_ref[0]
    a = -jnp.exp(alog_ref[...])
    row = lax.broadcasted_iota(jnp.int32, (cl, cl), 0)
    col = lax.broadcasted_iota(jnp.int32, (cl, cl), 1)
    tril = row >= col
    acum = jnp.dot(tril.astype(F32), dt * a, precision=_HIGHEST, preferred_element_type=F32)
    acum_t = acum.T
    lane_head = lax.broadcasted_iota(jnp.int32, (cl, LANES), 1) // headdim
    heads_per_blk = LANES // headdim
    bb = b_ref[0].astype(BF16)
    cc = c_ref[0].astype(BF16)
    cb = lax.dot_general(cc, bb, _NT, preferred_element_type=F32)
    h_t = h_scr[...]
    y_off = jnp.dot(cc, h_t.astype(BF16), preferred_element_type=F32)
    for j in range(gw // LANES):
        sl = slice(j * LANES, (j + 1) * LANES)
        h0 = j * heads_per_blk
        dte = jnp.broadcast_to(dt[:, h0:h0 + 1], (cl, LANES))
        ace = jnp.broadcast_to(acum[:, h0:h0 + 1], (cl, LANES))
        for q in range(1, heads_per_blk):
            dte = jnp.where(lane_head == q, dt[:, h0 + q:h0 + q + 1], dte)
            ace = jnp.where(lane_head == q, acum[:, h0 + q:h0 + q + 1], ace)
        xj = x[:, sl]
        xdt = xj * dte
        last = ace[cl - 1:cl, :]
        yd = jnp.zeros((cl, LANES), F32)
        for q in range(heads_per_blk):
            hh = h0 + q
            seg = acum[:, hh:hh + 1] - acum_t[hh:hh + 1, :]
            m = (cb * jnp.exp(jnp.where(tril, seg, -jnp.inf))).astype(BF16)
            xm = jnp.where(lane_head == q, xdt, 0.0).astype(BF16)
            yd = yd + jnp.dot(m, xm, preferred_element_type=F32)
        y = yd + y_off[:, sl] * jnp.exp(ace) + dsk_ref[:, sl] * xj
        y_scr[:, sl] = y * _silu(z_ref[0][:, sl])
        st_t = lax.dot_general(bb, (xdt * jnp.exp(last - ace)).astype(BF16), _TN,
                               preferred_element_type=F32)
        h_scr[:, sl] = h_t[:, sl] * jnp.exp(last) + st_t
    y_ref[0] = (_mean_sq_norm(y_scr[...]) * ng_ref[...]).astype(y_ref.dtype)

    @pl.when(ci == pl.num_programs(2) - 1)
    def _():
        hl_ref[0] = h_scr[...].T


def _ssd_prompt(xbc_act, dt_g, z, alog_g, dsk_e, norm_g, *, d_inner, groups, d_state, headdim):
    b, ln, _ = xbc_act.shape
    gw = d_inner // groups
    cl = min(SSD_CHUNK, ln)
    nc = ln // cl
    xb = d_inner // d_state
    return pl.pallas_call(
        functools.partial(_ssd_prompt_body, headdim=headdim),
        grid=(b, groups, nc),
        in_specs=[pl.BlockSpec((1, cl, gw), lambda i, g, c: (i, c, g)),
                  pl.BlockSpec((1, cl, d_state), lambda i, g, c: (i, c, xb + g)),
                  pl.BlockSpec((1, cl, d_state), lambda i, g, c: (i, c, xb + groups + g)),
                  pl.BlockSpec((1, cl, LANES), lambda i, g, c: (i, c, g)),
                  pl.BlockSpec((1, cl, gw), lambda i, g, c: (i, c, g)),
                  pl.BlockSpec((1, LANES), lambda i, g, c: (0, g)),
                  pl.BlockSpec((1, gw), lambda i, g, c: (0, g)),
                  pl.BlockSpec((1, gw), lambda i, g, c: (0, g))],
        out_specs=[pl.BlockSpec((1, cl, gw), lambda i, g, c: (i, c, g)),
                   pl.BlockSpec((1, gw, d_state), lambda i, g, c: (i, g, 0))],
        out_shape=[jax.ShapeDtypeStruct((b, ln, d_inner), BF16),
                   jax.ShapeDtypeStruct((b, d_inner, d_state), F32)],
        scratch_shapes=[pltpu.VMEM((d_state, gw), F32), pltpu.VMEM((cl, gw), F32)],
        compiler_params=_cparams("arbitrary", "arbitrary", "arbitrary"),
        name="ssd_prompt",
    )(xbc_act, xbc_act, xbc_act, dt_g, z, alog_g, dsk_e, norm_g)


def _ssd_sample_body(x_ref, dte_ref, b_ref, c_ref, z_ref, ae_ref, dsk_ref, ng_ref, h0_ref, y_ref, hn_ref,
                     *, groups, valid):
    x = x_ref[0]
    rows, di = x.shape
    gw = di // groups
    n = h0_ref.shape[2]
    rix = lax.broadcasted_iota(jnp.int32, (rows, di), 0)
    dte = jnp.where(rix < valid, dte_ref[0], 0.0)
    ac = dte * ae_ref[...]
    for sh in (1, 2, 4):
        ac = ac + jnp.where(rix >= sh, pltpu.roll(ac, sh, axis=0), 0.0)
    xdt = x * dte
    last = ac[rows - 1:rows, :]
    cdec = jnp.exp(last)
    hi = cdec.astype(BF16)
    r1 = cdec - hi.astype(F32)
    mid = r1.astype(BF16)
    lo = (r1 - mid.astype(F32)).astype(BF16)
    lhs = (xdt * jnp.exp(last - ac)).astype(BF16)
    lhs = jnp.where(rix == valid, hi, jnp.where(rix == valid + 1, mid, jnp.where(rix == valid + 2, lo, lhs)))
    rn = lax.broadcasted_iota(jnp.int32, (rows, n), 0)
    ones_rows = jnp.where((rn >= valid) & (rn < valid + 3), 1.0, 0.0).astype(BF16)
    rg = lax.broadcasted_iota(jnp.int32, (rows, gw), 0)
    eac = jnp.exp(ac)
    for g in range(groups):
        sl = slice(g * gw, (g + 1) * gw)
        bg = b_ref[0][:, g * n:(g + 1) * n]
        cg = c_ref[0][:, g * n:(g + 1) * n]
        h0 = h0_ref[0, sl, :]
        y_off = lax.dot_general(cg.astype(BF16), h0.astype(BF16), _NT, preferred_element_type=F32)
        rhs = jnp.concatenate([jnp.where(rn < valid, bg, 0.0).astype(BF16), ones_rows], axis=1)
        res = lax.dot_general(lhs[:, sl], rhs, _TN, preferred_element_type=F32)
        hn_ref[0, sl, :] = h0 * res[:, n:] + res[:, :n]
        acg = ac[:, sl]
        xdtg = xdt[:, sl]
        yd = jnp.zeros((rows, gw), F32)
        for s in range(valid):
            cbs = jnp.sum(cg * bg[s:s + 1, :], axis=-1, keepdims=True)
            w = jnp.where(rg >= s, jnp.exp(acg - acg[s:s + 1, :]), 0.0)
            yd = yd + cbs * w * xdtg[s:s + 1, :]
        xg = x[:, sl]
        y = (yd + y_off * eac[:, sl] + dsk_ref[:, sl] * xg) * _silu(z_ref[0][:, sl])
        y_ref[0, :, sl] = (_mean_sq_norm(y) * ng_ref[:, sl]).astype(y_ref.dtype)


def _ssd_sample(x_b, dte_b, bm_b, cm_b, z_b, a_e, dsk_e, norm_g, h0, *, groups, valid):
    b, rows, di = x_b.shape
    gn = bm_b.shape[2]
    n = h0.shape[2]
    assert valid + 3 <= rows
    seq = lambda i: (i, 0, 0)
    par = lambda i: (0, 0)
    return pl.pallas_call(
        functools.partial(_ssd_sample_body, groups=groups, valid=valid),
        grid=(b,),
        in_specs=[pl.BlockSpec((1, rows, di), seq),
                  pl.BlockSpec((1, rows, di), seq),
                  pl.BlockSpec((1, rows, gn), seq),
                  pl.BlockSpec((1, rows, gn), seq),
                  pl.BlockSpec((1, rows, di), seq),
                  pl.BlockSpec((1, di), par),
                  pl.BlockSpec((1, di), par),
                  pl.BlockSpec((1, di), par),
                  pl.BlockSpec((1, di, n), seq)],
        out_specs=[pl.BlockSpec((1, rows, di), seq),
                   pl.BlockSpec((1, di, n), seq)],
        out_shape=[jax.ShapeDtypeStruct((b, rows, di), F32),
                   jax.ShapeDtypeStruct((b, di, n), F32)],
        compiler_params=_cparams("arbitrary"),
        name="ssd_sample",
    )(x_b, dte_b, bm_b, cm_b, z_b, a_e, dsk_e, norm_g, h0)


def _layernorm(v, g, b):
    mu = jnp.mean(v, axis=-1, keepdims=True)
    d = v - mu
    var = jnp.mean(d * d, axis=-1, keepdims=True)
    return d * lax.rsqrt(var + EPS) * g + b


def _gmlp_prompt_body(gv_ref, gu_ref, lg_ref, lb_ref, ws_ref, bst_ref, o_ref):
    vn = _layernorm(gv_ref[0], lg_ref[...], lb_ref[...])
    cl, d = vn.shape
    groups = ws_ref.shape[0]
    gw = d // groups
    row = lax.broadcasted_iota(jnp.int32, (cl, cl), 0)
    col = lax.broadcasted_iota(jnp.int32, (cl, cl), 1)
    tril = row >= col
    for g in range(groups):
        sl = slice(g * gw, (g + 1) * gw)
        w = jnp.where(tril, ws_ref[g], 0.0).astype(BF16)
        s = jnp.dot(w, vn[:, sl].astype(BF16), preferred_element_type=F32) + bst_ref[:, g:g + 1]
        o_ref[0, :, sl] = (gu_ref[0][:, sl] * s).astype(o_ref.dtype)


def _gmlp_prompt(guv, ln_g, ln_b, ws, bs_t):
    b, ln, d2 = guv.shape
    d = d2 // 2
    cl = min(GMLP_CHUNK, ln)
    groups = ws.shape[0]
    return pl.pallas_call(
        _gmlp_prompt_body,
        grid=(b, ln // cl),
        in_specs=[pl.BlockSpec((1, cl, d), lambda i, c: (i, c, 1)),
                  pl.BlockSpec((1, cl, d), lambda i, c: (i, c, 0)),
                  pl.BlockSpec((1, d), lambda i, c: (0, 0)),
                  pl.BlockSpec((1, d), lambda i, c: (0, 0)),
                  pl.BlockSpec((groups, cl, cl), lambda i, c: (0, 0, 0)),
                  pl.BlockSpec((cl, groups), lambda i, c: (0, 0))],
        out_specs=pl.BlockSpec((1, cl, d), lambda i, c: (i, c, 0)),
        out_shape=jax.ShapeDtypeStruct((b, ln, d), BF16),
        compiler_params=_cparams("arbitrary", "arbitrary"),
        name="gmlp_prompt",
    )(guv, guv, ln_g, ln_b, ws, bs_t)


def _gmlp_slab_body(gv_ref, gu_ref, lg_ref, lb_ref, wrow_ref, brow_ref, o_ref, vn_ref):
    ln = gv_ref.shape[0]
    vn = [_layernorm(gv_ref[t], lg_ref[...], lb_ref[...]) for t in range(ln)]
    for t in range(ln):
        vn_ref[t] = vn[t]
        s = brow_ref[t:t + 1, :] + vn[0] * wrow_ref[t * ln:t * ln + 1, :]
        for k in range(1, t + 1):
            s = s + vn[k] * wrow_ref[t * ln + k:t * ln + k + 1, :]
        o_ref[t] = (gu_ref[t] * s).astype(o_ref.dtype)


def _gmlp_slab(guv, ln_g, ln_b, wrow, brow):
    ln, b, d2 = guv.shape
    d = d2 // 2
    bt = _row_tile(b, 32)
    return pl.pallas_call(
        _gmlp_slab_body,
        grid=(b // bt,),
        in_specs=[pl.BlockSpec((ln, bt, d), lambda i: (0, i, 1)),
                  pl.BlockSpec((ln, bt, d), lambda i: (0, i, 0)),
                  pl.BlockSpec((1, d), lambda i: (0, 0)),
                  pl.BlockSpec((1, d), lambda i: (0, 0)),
                  pl.BlockSpec((ln * ln, d), lambda i: (0, 0)),
                  pl.BlockSpec((ln, d), lambda i: (0, 0))],
        out_specs=[pl.BlockSpec((ln, bt, d), lambda i: (0, i, 0)),
                   pl.BlockSpec((ln, bt, d), lambda i: (0, i, 0))],
        out_shape=[jax.ShapeDtypeStruct((ln, b, d), BF16),
                   jax.ShapeDtypeStruct((ln, b, d), F32)],
        compiler_params=_cparams("arbitrary"),
        name="gmlp_sample",
    )(guv, guv, ln_g, ln_b, wrow, brow)


def _peer_scores_body(h_ref, wq_ref, k_ref, o_ref):
    q = jnp.dot(h_ref[...], wq_ref[...], preferred_element_type=F32).astype(BF16)
    half = k_ref.shape[3]
    for i in range(2):
        o_ref[0, i] = lax.dot_general(k_ref[0, i], q[:, i * half:(i + 1) * half], _NT,
                                      preferred_element_type=F32)


def _peer_scores(h2, wq, keys, tt):
    m, d = h2.shape
    heads, _, nk, half = keys.shape
    return pl.pallas_call(
        _peer_scores_body,
        grid=(heads, m // tt),
        in_specs=[pl.BlockSpec((tt, d), lambda h, t: (t, 0)),
                  pl.BlockSpec((d, 2 * half), lambda h, t: (0, h)),
                  pl.BlockSpec((1, 2, nk, half), lambda h, t: (h, 0, 0, 0))],
        out_specs=pl.BlockSpec((1, 2, nk, tt), lambda h, t: (h, 0, 0, t)),
        out_shape=jax.ShapeDtypeStruct((heads, 2, nk, m), F32),
        compiler_params=_cparams("arbitrary", "arbitrary"),
        name="peer_scores",
    )(h2, wq, keys)


def _top_values(s, k):
    n = s.shape[0]
    ridx = lax.broadcasted_iota(jnp.int32, s.shape, 0)
    kidx = lax.broadcasted_iota(jnp.int32, (k, s.shape[1]), 0)
    out = jnp.zeros((k, s.shape[1]), F32)
    for r in range(k):
        m = jnp.max(s, axis=0, keepdims=True)
        first = jnp.min(jnp.where(s == m, ridx, n), axis=0, keepdims=True)
        s = jnp.where(ridx == first, -jnp.inf, s)
        out = jnp.where(kidx == r, m, out)
    return out


def _peer_stats_body(s_ref, o_ref, *, topk):
    heads = s_ref.shape[0]
    for h in range(heads):
        v1 = _top_values(s_ref[h, 0], topk)
        v2 = _top_values(s_ref[h, 1], topk)
        cand = jnp.concatenate([v1[i:i + 1, :] + v2 for i in range(topk)], axis=0)
        top = _top_values(cand, topk)
        mx = top[0:1, :]
        lse = mx + jnp.log(jnp.sum(jnp.exp(top - mx), axis=0, keepdims=True))
        o_ref[h, 0:1, :] = top[topk - 1:topk, :]
        o_ref[h, 1:2, :] = lse


def _peer_stats(scores, tt):
    heads, _, nk, m = scores.shape
    return pl.pallas_call(
        functools.partial(_peer_stats_body, topk=PEER_TOPK),
        grid=(m // tt,),
        in_specs=[pl.BlockSpec((heads, 2, nk, tt), lambda t: (0, 0, 0, t))],
        out_specs=pl.BlockSpec((heads, 2, tt), lambda t: (0, 0, t)),
        out_shape=jax.ShapeDtypeStruct((heads, 2, m), F32),
        compiler_params=_cparams("arbitrary"),
        name="peer_stats",
    )(scores)


def _peer_mix_body(h_ref, u_ref, vt_ref, s_ref, st_ref, o_ref, *, nk):
    e = pl.program_id(1)

    @pl.when(e == 0)
    def _():
        o_ref[...] = jnp.zeros_like(o_ref)

    et = u_ref.shape[0]
    heads = s_ref.shape[0]
    act = _gelu(lax.dot_general(u_ref[...], h_ref[...], _NT, preferred_element_type=F32))
    parts = []
    for ii in range(et // nk):
        i = e * (et // nk) + ii
        wsum = jnp.zeros((nk, act.shape[1]), F32)
        for h in range(heads):
            c = s_ref[h, 0, pl.ds(i, 1), :] + s_ref[h, 1]
            wsum = wsum + jnp.where(c >= st_ref[h, 0:1, :], jnp.exp(c - st_ref[h, 1:2, :]), 0.0)
        parts.append((wsum * act[ii * nk:(ii + 1) * nk, :]).astype(BF16))
    wg = jnp.concatenate(parts, axis=0) if len(parts) > 1 else parts[0]
    o_ref[...] += jnp.dot(vt_ref[...], wg, preferred_element_type=F32)


def _peer_mix(h2, u, vt, scores, stats, tt, et):
    m, d = h2.shape
    ne = u.shape[0]
    heads, _, nk, _ = scores.shape
    return pl.pallas_call(
        functools.partial(_peer_mix_body, nk=nk),
        grid=(m // tt, ne // et),
        in_specs=[pl.BlockSpec((tt, d), lambda t, e: (t, 0)),
                  pl.BlockSpec((et, d), lambda t, e: (e, 0)),
                  pl.BlockSpec((d, et), lambda t, e: (0, e)),
                  pl.BlockSpec((heads, 2, nk, tt), lambda t, e: (0, 0, 0, t)),
                  pl.BlockSpec((heads, 2, tt), lambda t, e: (0, 0, t))],
        out_specs=pl.BlockSpec((d, tt), lambda t, e: (0, t)),
        out_shape=jax.ShapeDtypeStruct((d, m), F32),
        compiler_params=_cparams("arbitrary", "arbitrary"),
        name="peer_mix",
    )(h2, u, vt, scores, stats)


def _final_body(x_ref, pt_ref, gate_ref, fg_ref, o_ref):
    xn = x_ref[0] + gate_ref[0] * pt_ref[...].T
    o_ref[0] = _mean_sq_norm(xn) * fg_ref[...]


def _final(x3, peer_t, col0, mod, k_gate, final_g):
    gg, r, d = x3.shape
    tm = _row_tile(r, 256)
    nb = r // tm
    c0 = col0 // tm
    return pl.pallas_call(
        _final_body,
        grid=(gg, nb),
        in_specs=[pl.BlockSpec((1, tm, d), lambda b, i: (b, i, 0)),
                  pl.BlockSpec((d, tm), lambda b, i: (0, c0 + b * nb + i)),
                  _mod_spec(mod, k_gate, tm, d, 2),
                  pl.BlockSpec((1, d), lambda b, i: (0, 0))],
        out_specs=pl.BlockSpec((1, tm, d), lambda b, i: (b, i, 0)),
        out_shape=jax.ShapeDtypeStruct((gg, r, d), F32),
        compiler_params=_cparams("arbitrary", "arbitrary"),
        name="final",
    )(x3, peer_t, mod, final_g)


def _mixer(x3, mod, w, dims, *, sample, ssm0=None, conv0=None):
    d, d_inner, groups, d_state, headdim, conv_dim = dims
    heads = d_inner // headdim
    hpg = heads // groups
    h = _norm_mod(x3, w["norm1_g"], mod, 0, 1)
    z = _matmul(h, w["w_z"], _ep_none, name="in_z")
    xbc = _matmul(h, w["w_xbc"], _ep_none, name="in_xbc")
    dt = _matmul(h, w["w_dt"], _ep_softplus_bias, extras=[(w["dt_bias"], "bias")], name="in_dt")
    guv = _matmul(h, w["w_uv"], _ep_gelu, name="in_uv")
    sg = _matmul(h, w["w_g"], _ep_sigmoid, name="in_gates")

    if not sample:
        b, ln, _ = x3.shape
        xbc_act = _conv_prompt(xbc, w["conv_w"], w["conv_b"])
        conv_new = xbc[:, ln - (w["conv_w"].shape[0] - 1):, :]
        dt_g = jnp.pad(dt.reshape(b, ln, groups, hpg), ((0, 0), (0, 0), (0, 0), (0, LANES - hpg)))
        dt_g = dt_g.reshape(b, ln, groups * LANES)
        yn, h_new = _ssd_prompt(xbc_act, dt_g, z, w["alog_g"], w["dsk_e"], w["ssd_norm_g"],
                                d_inner=d_inner, groups=groups, d_state=d_state, headdim=headdim)
        ssm_new = h_new.reshape(b, heads, headdim, d_state)
        yb = _gmlp_prompt(guv, w["gmlp_ln_g"], w["gmlp_ln_b"], w["gmlp_ws"], w["gmlp_bs_t"])
        v_rows = None
    else:
        ln, b, _ = x3.shape
        cw = w["conv_w"].shape[0]
        xp = jnp.concatenate([jnp.transpose(conv0, (1, 0, 2)), xbc], axis=0)
        xbc_act = _conv_slab(xp, w["conv_w"], w["conv_b"])
        conv_new = jnp.transpose(xp[ln:ln + cw - 1], (1, 0, 2))

        def to_rows(t):
            return jnp.pad(jnp.transpose(t, (1, 0, 2)), ((0, 0), (0, SUBLANES - ln), (0, 0)))

        act_b = to_rows(xbc_act)
        dte_b = jnp.repeat(to_rows(dt), headdim, axis=2)
        yn_b, h_new = _ssd_sample(act_b[:, :, :d_inner], dte_b, act_b[:, :, d_inner:d_inner + groups * d_state],
                                  act_b[:, :, d_inner + groups * d_state:], to_rows(z), w["a_e"], w["dsk_e"],
                                  w["ssd_norm_g"], ssm0.reshape(b, d_inner, d_state), groups=groups, valid=ln)
        ssm_new = h_new.reshape(b, heads, headdim, d_state)
        yn = jnp.transpose(yn_b[:, :ln], (1, 0, 2)).astype(BF16)
        wrow, brow = _slab_gate_rows(w, ln)
        yb, v_tm = _gmlp_slab(guv, w["gmlp_ln_g"], w["gmlp_ln_b"], wrow, brow)
        v_rows = jnp.transpose(v_tm, (1, 0, 2))

    y_a = _matmul(yn, w["w_oa"], _ep_none, name="out_a")
    merged = _matmul(yb, w["w_ob"], _ep_merge, extras=[(y_a, "tile"), (sg, "tile", 0), (sg, "tile", 1)],
                     out_dtype=BF16, name="out_b_merge")
    x1 = _matmul(merged, w["w_out"], _ep_residual, extras=[(x3, "tile"), (mod, "mod", 2)], name="out_proj")
    h2 = _norm_mod(x1, w["norm2_g"], mod, 3, 4)
    return x1, h2, ssm_new, conv_new, v_rows


def _slab_gate_rows(w, ln):
    ws, bs = w["gmlp_ws"], w["gmlp_bs"]
    gw = w["gmlp_ln_g"].shape[1] // ws.shape[0]
    wt = jnp.where(jnp.tril(jnp.ones((ln, ln), bool)), ws[:, :ln, :ln], 0.0)
    wrow = jnp.repeat(jnp.transpose(wt, (1, 2, 0)).reshape(ln * ln, -1), gw, axis=1)
    brow = jnp.repeat(jnp.transpose(bs[:, :ln]), gw, axis=1)
    return wrow, brow


def kernel(x_prompt, x_sample, state_ssm, state_conv, c_prompt, c_sample, w_ada, b_ada, norm1_g, w_in, conv_w, conv_b, dt_bias, a_log, d_skip, ssd_norm_g, w_oa, gmlp_ln_g, gmlp_ln_b, gmlp_ws, gmlp_bs, w_ob, w_out, norm2_g, peer_wq, peer_keys, peer_u, peer_v, final_g):
    depth = w_ada.shape[0]
    bp, lp, d = x_prompt.shape
    bs_, ls, _ = x_sample.shape
    heads = a_log.shape[1]
    headdim, d_state = state_ssm.shape[3], state_ssm.shape[4]
    d_inner = heads * headdim
    conv_dim = conv_w.shape[2]
    groups = (conv_dim - d_inner) // (2 * d_state)
    hpg = heads // groups
    d_gmlp = gmlp_ln_g.shape[1]
    dims = (d, d_inner, groups, d_state, headdim, conv_dim)
    off_xbc = d_inner
    off_dt = off_xbc + conv_dim
    off_u = off_dt + heads
    off_ga = off_u + 2 * d_gmlp

    xp = x_prompt
    xs = jnp.transpose(x_sample, (1, 0, 2))
    n_c = bs_ + bp
    c_all = jnp.pad(jnp.concatenate([c_sample, c_prompt], axis=0), ((0, (-n_c) % SUBLANES), (0, 0)))
    ssm_p, conv_p, ssm_s, conv_s, v_s = [], [], [], [], []
    for l in range(depth):
        mod = _ada(c_all, w_ada[l], b_ada[l][None, :])
        mod_s = mod[:bs_][None]
        mod_p = mod[bs_:n_c][:, None, :]
        wl = w_in[l]
        a = -jnp.exp(a_log[l])
        w = dict(
            norm1_g=norm1_g[l][None, :], norm2_g=norm2_g[l][None, :],
            w_z=wl[:, :off_xbc].astype(BF16), w_xbc=wl[:, off_xbc:off_dt].astype(BF16),
            w_dt=wl[:, off_dt:off_u].astype(BF16), w_uv=wl[:, off_u:off_ga].astype(BF16),
            w_g=wl[:, off_ga:].astype(BF16), dt_bias=dt_bias[l][None, :],
            conv_w=conv_w[l], conv_b=conv_b[l][None, :],
            alog_g=jnp.pad(a_log[l].reshape(groups, hpg), ((0, 0), (0, LANES - hpg))).reshape(1, groups * LANES),
            a_e=jnp.repeat(a, headdim)[None, :], dsk_e=jnp.repeat(d_skip[l], headdim)[None, :],
            ssd_norm_g=ssd_norm_g[l][None, :],
            w_oa=w_oa[l].astype(BF16), w_ob=w_ob[l].astype(BF16), w_out=w_out[l].astype(BF16),
            gmlp_ln_g=gmlp_ln_g[l][None, :], gmlp_ln_b=gmlp_ln_b[l][None, :],
            gmlp_ws=gmlp_ws[l], gmlp_bs=gmlp_bs[l], gmlp_bs_t=jnp.transpose(gmlp_bs[l]),
        )
        x1p, h2p, hp, cp, _ = _mixer(xp, mod_p, w, dims, sample=False)
        x1s, h2s, hs, cs, vs = _mixer(xs, mod_s, w, dims, sample=True, ssm0=state_ssm[l], conv0=state_conv[l])

        h2 = jnp.concatenate([h2p.reshape(bp * lp, d), h2s.reshape(ls * bs_, d)], axis=0)
        m = h2.shape[0]
        tt = _row_tile(m, 512)
        ne = peer_u.shape[1]
        et = _row_tile(ne, 256)
        scores = _peer_scores(h2, peer_wq[l].astype(BF16), peer_keys[l].astype(BF16), tt)
        stats = _peer_stats(scores, tt)
        peer_t = _peer_mix(h2, peer_u[l].astype(BF16), jnp.transpose(peer_v[l]).astype(BF16), scores, stats, tt, et)
        last = l == depth - 1
        fg = final_g[None, :] if last else jnp.ones((1, d), F32)
        assert last, "only DEPTH == 1 is supported"
        xp = _final(x1p, peer_t, 0, mod_p, 5, fg)
        xs = _final(x1s, peer_t, bp * lp, mod_s, 5, fg)
        ssm_p.append(hp)
        conv_p.append(cp)
        ssm_s.append(hs)
        conv_s.append(cs)
        v_s.append(vs)
    y_prompt = xp
    y_sample = jnp.transpose(xs, (1, 0, 2))
    return (y_prompt, y_sample, jnp.stack(ssm_p), jnp.stack(conv_p), jnp.stack(ssm_s), jnp.stack(conv_s),
            jnp.stack(v_s))
```

```python
import functools

import jax
import jax.numpy as jnp
from jax import lax
from jax.experimental import pallas as pl
from jax.experimental.pallas import tpu as pltpu

F32 = jnp.float32
BF16 = jnp.bfloat16
EPS = 1e-6
N_MOD = 6
PEER_TOPK = 16
SSD_CHUNK = 128
GMLP_CHUNK = 128
SUBLANES = 8
LANES = 128
VMEM_LIMIT = 56 * 1024 * 1024
MATMUL_VMEM_BUDGET = 40 * 1024 * 1024
_SQRT_HALF = 0.7071067811865476
_HIGHEST = lax.Precision.HIGHEST
_NT = (((1,), (1,)), ((), ()))
_TN = (((0,), (0,)), ((), ()))


def _cparams(*sem):
    return pltpu.CompilerParams(dimension_semantics=sem, vmem_limit_bytes=VMEM_LIMIT)


def _silu(x):
    return x * jax.nn.sigmoid(x)


def _gelu(x):
    return 0.5 * x * (1.0 + lax.erf(x * _SQRT_HALF))


def _softplus(x):
    return jnp.maximum(x, 0.0) + jnp.log1p(jnp.exp(-jnp.abs(x)))


def _mean_sq_norm(x):
    return x * lax.rsqrt(jnp.mean(x * x, axis=-1, keepdims=True) + EPS)


def _ada_body(c_ref, w_ref, b_ref, o_ref):
    a = _silu(c_ref[...]).astype(BF16)
    o_ref[...] = jnp.dot(a, w_ref[...].astype(BF16), preferred_element_type=F32) + b_ref[...]


def _ada(c_all, w_ada, b_ada):
    m, d = c_all.shape
    n = w_ada.shape[1]
    tn = min(512, n)
    return pl.pallas_call(
        _ada_body,
        grid=(n // tn,),
        in_specs=[pl.BlockSpec((m, d), lambda j: (0, 0)),
                  pl.BlockSpec((d, tn), lambda j: (0, j)),
                  pl.BlockSpec((1, tn), lambda j: (0, j))],
        out_specs=pl.BlockSpec((m, tn), lambda j: (0, j)),
        out_shape=jax.ShapeDtypeStruct((m, n), F32),
        compiler_params=_cparams("arbitrary"),
        name="ada",
    )(c_all, w_ada, b_ada)


def _mod_spec(mod, k, tm, tn, grid_rank):
    per_seq = mod.shape[1] == 1
    nblk = mod.shape[2] // N_MOD // tn

    if grid_rank == 2:
        if per_seq:
            return pl.BlockSpec((1, 1, tn), lambda g, i: (g, 0, k))
        return pl.BlockSpec((1, tm, tn), lambda g, i: (0, i, k))
    if per_seq:
        return pl.BlockSpec((1, 1, tn), lambda j, g, i: (g, 0, k * nblk + j))
    return pl.BlockSpec((1, tm, tn), lambda j, g, i: (0, i, k * nblk + j))


def _row_tile(r, target):
    t = min(r, target)
    while r % t:
        t //= 2
    return t


def _norm_mod_body(x_ref, g_ref, sc_ref, sh_ref, o_ref):
    y = _mean_sq_norm(x_ref[0]) * g_ref[...]
    o_ref[0] = (y * (1.0 + sc_ref[0]) + sh_ref[0]).astype(o_ref.dtype)


def _norm_mod(x3, g, mod, k_shift, k_scale):
    gg, r, d = x3.shape
    tm = _row_tile(r, 512)
    return pl.pallas_call(
        _norm_mod_body,
        grid=(gg, r // tm),
        in_specs=[pl.BlockSpec((1, tm, d), lambda b, i: (b, i, 0)),
                  pl.BlockSpec((1, d), lambda b, i: (0, 0)),
                  _mod_spec(mod, k_scale, tm, d, 2),
                  _mod_spec(mod, k_shift, tm, d, 2)],
        out_specs=pl.BlockSpec((1, tm, d), lambda b, i: (b, i, 0)),
        out_shape=jax.ShapeDtypeStruct((gg, r, d), BF16),
        compiler_params=_cparams("arbitrary", "arbitrary"),
        name="norm_mod",
    )(x3, g, mod, mod)


def _mm_body(*refs, epilogue, kinds):
    a_ref, w_ref = refs[0], refs[1]
    o_ref = refs[-1]
    acc = jnp.dot(a_ref[0], w_ref[...], preferred_element_type=F32)
    ex = [r[...] if kd == "bias" else r[0] for r, kd in zip(refs[2:-1], kinds)]
    o_ref[0] = epilogue(acc, *ex).astype(o_ref.dtype)


def _matmul(a3, w, epilogue, extras=(), out_dtype=F32, tm_target=1024, tn_target=1024, name="mm"):
    gg, r, k = a3.shape
    n = w.shape[1]
    tm = _row_tile(r, tm_target)
    tn = _row_tile(n, tn_target)
    tile_bytes = sum(e[0].dtype.itemsize for e in extras if e[1] == "tile") + jnp.dtype(out_dtype).itemsize

    def vmem_bytes(tm_, tn_):
        return 2 * (tm_ * k * 2 + k * tn_ * 2 + tm_ * tn_ * tile_bytes) + 2 * tm_ * tn_ * 4

    while vmem_bytes(tm, tn) > MATMUL_VMEM_BUDGET and (tm > 256 or tn > 256):
        if tm >= tn and tm > 256:
            tm //= 2
        else:
            tn //= 2
    in_specs = [pl.BlockSpec((1, tm, k), lambda j, g, i: (g, i, 0)),
                pl.BlockSpec((k, tn), lambda j, g, i: (0, j))]
    args = [a3, w]
    kinds = []
    for e in extras:
        arr, kind = e[0], e[1]
        kinds.append(kind)
        args.append(arr)
        if kind == "tile":
            off = e[2] if len(e) > 2 else 0
            nb = n // tn
            in_specs.append(pl.BlockSpec((1, tm, tn), lambda j, g, i, off=off, nb=nb: (g, i, off * nb + j)))
        elif kind == "bias":
            in_specs.append(pl.BlockSpec((1, tn), lambda j, g, i: (0, j)))
        else:
            in_specs.append(_mod_spec(arr, e[2], tm, tn, 3))
    return pl.pallas_call(
        functools.partial(_mm_body, epilogue=epilogue, kinds=tuple(kinds)),
        grid=(n // tn, gg, r // tm),
        in_specs=in_specs,
        out_specs=pl.BlockSpec((1, tm, tn), lambda j, g, i: (g, i, j)),
        out_shape=jax.ShapeDtypeStruct((gg, r, n), out_dtype),
        compiler_params=_cparams("arbitrary", "arbitrary", "arbitrary"),
        name=name,
    )(*args)


def _ep_none(acc):
    return acc


def _ep_gelu(acc):
    return _gelu(acc)


def _ep_sigmoid(acc):
    return jax.nn.sigmoid(acc)


def _ep_softplus_bias(acc, bias):
    return _softplus(acc + bias)


def _ep_merge(acc, y_a, sg_a, sg_b):
    return sg_a * y_a + sg_b * acc


def _ep_residual(acc, x, gate):
    return x + gate * acc


def _conv_prompt_body(x_ref, w_ref, b_ref, o_ref, pad_ref, *, width):
    ln = x_ref.shape[1]
    pad_ref[0:SUBLANES, :] = jnp.zeros((SUBLANES, pad_ref.shape[1]), F32)
    pad_ref[SUBLANES:SUBLANES + ln, :] = x_ref[0]
    acc = b_ref[...] + x_ref[0] * w_ref[width - 1:width, :]
    for k in range(width - 1):
        s = SUBLANES - (width - 1) + k
        acc = acc + pad_ref[s:s + ln, :] * w_ref[k:k + 1, :]
    o_ref[0] = _silu(acc)


def _conv_prompt(x3, conv_w, conv_b):
    b, ln, c = x3.shape
    width = conv_w.shape[0]
    ct = _row_tile(c, 512)
    return pl.pallas_call(
        functools.partial(_conv_prompt_body, width=width),
        grid=(b, c // ct),
        in_specs=[pl.BlockSpec((1, ln, ct), lambda i, j: (i, 0, j)),
                  pl.BlockSpec((width, ct), lambda i, j: (0, j)),
                  pl.BlockSpec((1, ct), lambda i, j: (0, j))],
        out_specs=pl.BlockSpec((1, ln, ct), lambda i, j: (i, 0, j)),
        out_shape=jax.ShapeDtypeStruct((b, ln, c), F32),
        scratch_shapes=[pltpu.VMEM((ln + SUBLANES, ct), F32)],
        compiler_params=_cparams("arbitrary", "arbitrary"),
        name="conv_prompt",
    )(x3, conv_w, conv_b)


def _conv_slab_body(xp_ref, w_ref, b_ref, o_ref, *, width):
    for t in range(o_ref.shape[0]):
        acc = b_ref[...] + xp_ref[t] * w_ref[0:1, :]
        for k in range(1, width):
            acc = acc + xp_ref[t + k] * w_ref[k:k + 1, :]
        o_ref[t] = _silu(acc)


def _conv_slab(xp, conv_w, conv_b):
    lp, b, c = xp.shape
    width = conv_w.shape[0]
    ln = lp - width + 1
    ct = _row_tile(c, 1024)
    return pl.pallas_call(
        functools.partial(_conv_slab_body, width=width),
        grid=(c // ct,),
        in_specs=[pl.BlockSpec((lp, b, ct), lambda j: (0, 0, j)),
                  pl.BlockSpec((width, ct), lambda j: (0, j)),
                  pl.BlockSpec((1, ct), lambda j: (0, j))],
        out_specs=pl.BlockSpec((ln, b, ct), lambda j: (0, 0, j)),
        out_shape=jax.ShapeDtypeStruct((ln, b, c), F32),
        compiler_params=_cparams("arbitrary"),
        name="conv_sample",
    )(xp, conv_w, conv_b)


def _ssd_prompt_body(x_ref, b_ref, c_ref, dt_ref, z_ref, alog_ref, dsk_ref, ng_ref, y_ref, hl_ref,
                     h_scr, y_scr, *, headdim):
    ci = pl.program_id(2)

    @pl.when(ci == 0)
    def _():
        h_scr[...] = jnp.zeros_like(h_scr)

    x = x_ref[0]
    cl, gw = x.shape
    dt = dt_ref[0]
    a = -jnp.exp(alog_ref[...])
    row = lax.broadcasted_iota(jnp.int32, (cl, cl), 0)
    col = lax.broadcasted_iota(jnp.int32, (cl, cl), 1)
    tril = row >= col
    acum = jnp.dot(tril.astype(F32), dt * a, precision=_HIGHEST, preferred_element_type=F32)
    acum_t = acum.T
    lane_head = lax.broadcasted_iota(jnp.int32, (cl, LANES), 1) // headdim
    heads_per_blk = LANES // headdim
    bb = b_ref[0].astype(BF16)
    cc = c_ref[0].astype(BF16)
    cb = lax.dot_general(cc, bb, _NT, preferred_element_type=F32)
    h_t = h_scr[...]
    y_off = jnp.dot(cc, h_t.astype(BF16), preferred_element_type=F32)
    for j in range(gw // LANES):
        sl = slice(j * LANES, (j + 1) * LANES)
        h0 = j * heads_per_blk
        dte = jnp.broadcast_to(dt[:, h0:h0 + 1], (cl, LANES))
        ace = jnp.broadcast_to(acum[:, h0:h0 + 1], (cl, LANES))
        for q in range(1, heads_per_blk):
            dte = jnp.where(lane_head == q, dt[:, h0 + q:h0 + q + 1], dte)
            ace = jnp.where(lane_head == q, acum[:, h0 + q:h0 + q + 1], ace)
        xj = x[:, sl]
        xdt = xj * dte
        last = ace[cl - 1:cl, :]
        yd = jnp.zeros((cl, LANES), F32)
        for q in range(heads_per_blk):
            hh = h0 + q
            seg = acum[:, hh:hh + 1] - acum_t[hh:hh + 1, :]
            m = (cb * jnp.exp(jnp.where(tril, seg, -jnp.inf))).astype(BF16)
            xm = jnp.where(lane_head == q, xdt, 0.0).astype(BF16)
            yd = yd + jnp.dot(m, xm, preferred_element_type=F32)
        y = yd + y_off[:, sl] * jnp.exp(ace) + dsk_ref[:, sl] * xj
        y_scr[:, sl] = y * _silu(z_ref[0][:, sl])
        st_t = lax.dot_general(bb, (xdt * jnp.exp(last - ace)).astype(BF16), _TN,
                               preferred_element_type=F32)
        h_scr[:, sl] = h_t[:, sl] * jnp.exp(last) + st_t
    y_ref[0] = (_mean_sq_norm(y_scr[...]) * ng_ref[...]).astype(y_ref.dtype)

    @pl.when(ci == pl.num_programs(2) - 1)
    def _():
        hl_ref[0] = h_scr[...].T


def _ssd_prompt(xbc_act, dt_g, z, alog_g, dsk_e, norm_g, *, d_inner, groups, d_state, headdim):
    b, ln, _ = xbc_act.shape
    gw = d_inner // groups
    cl = min(SSD_CHUNK, ln)
    nc = ln // cl
    xb = d_inner // d_state
    return pl.pallas_call(
        functools.partial(_ssd_prompt_body, headdim=headdim),
        grid=(b, groups, nc),
        in_specs=[pl.BlockSpec((1, cl, gw), lambda i, g, c: (i, c, g)),
                  pl.BlockSpec((1, cl, d_state), lambda i, g, c: (i, c, xb + g)),
                  pl.BlockSpec((1, cl, d_state), lambda i, g, c: (i, c, xb + groups + g)),
                  pl.BlockSpec((1, cl, LANES), lambda i, g, c: (i, c, g)),
                  pl.BlockSpec((1, cl, gw), lambda i, g, c: (i, c, g)),
                  pl.BlockSpec((1, LANES), lambda i, g, c: (0, g)),
                  pl.BlockSpec((1, gw), lambda i, g, c: (0, g)),
                  pl.BlockSpec((1, gw), lambda i, g, c: (0, g))],
        out_specs=[pl.BlockSpec((1, cl, gw), lambda i, g, c: (i, c, g)),
                   pl.BlockSpec((1, gw, d_state), lambda i, g, c: (i, g, 0))],
        out_shape=[jax.ShapeDtypeStruct((b, ln, d_inner), BF16),
                   jax.ShapeDtypeStruct((b, d_inner, d_state), F32)],
        scratch_shapes=[pltpu.VMEM((d_state, gw), F32), pltpu.VMEM((cl, gw), F32)],
        compiler_params=_cparams("arbitrary", "arbitrary", "arbitrary"),
        name="ssd_prompt",
    )(xbc_act, xbc_act, xbc_act, dt_g, z, alog_g, dsk_e, norm_g)


def _ssd_sample_body(x_ref, dte_ref, b_ref, c_ref, z_ref, ae_ref, dsk_ref, ng_ref, h0_ref, y_ref, hn_ref,
                     *, groups, valid):
    x = x_ref[0]
    rows, di = x.shape
    gw = di // groups
    n = h0_ref.shape[2]
    rix = lax.broadcasted_iota(jnp.int32, (rows, di), 0)
    dte = jnp.where(rix < valid, dte_ref[0], 0.0)
    ac = dte * ae_ref[...]
    for sh in (1, 2, 4):
        ac = ac + jnp.where(rix >= sh, pltpu.roll(ac, sh, axis=0), 0.0)
    xdt = x * dte
    last = ac[rows - 1:rows, :]
    cdec = jnp.exp(last)
    hi = cdec.astype(BF16)
    r1 = cdec - hi.astype(F32)
    mid = r1.astype(BF16)
    lo = (r1 - mid.astype(F32)).astype(BF16)
    lhs = (xdt * jnp.exp(last - ac)).astype(BF16)
    lhs = jnp.where(rix == valid, hi, jnp.where(rix == valid + 1, mid, jnp.where(rix == valid + 2, lo, lhs)))
    rn = lax.broadcasted_iota(jnp.int32, (rows, n), 0)
    ones_rows = jnp.where((rn >= valid) & (rn < valid + 3), 1.0, 0.0).astype(BF16)
    rg = lax.broadcasted_iota(jnp.int32, (rows, gw), 0)
    eac = jnp.exp(ac)
    for g in range(groups):
        sl = slice(g * gw, (g + 1) * gw)
        bg = b_ref[0][:, g * n:(g + 1) * n]
        cg = c_ref[0][:, g * n:(g + 1) * n]
        h0 = h0_ref[0, sl, :]
        y_off = lax.dot_general(cg.astype(BF16), h0.astype(BF16), _NT, preferred_element_type=F32)
        rhs = jnp.concatenate([jnp.where(rn < valid, bg, 0.0).astype(BF16), ones_rows], axis=1)
        res = lax.dot_general(lhs[:, sl], rhs, _TN, preferred_element_type=F32)
        hn_ref[0, sl, :] = h0 * res[:, n:] + res[:, :n]
        acg = ac[:, sl]
        xdtg = xdt[:, sl]
        yd = jnp.zeros((rows, gw), F32)
        for s in range(valid):
            cbs = jnp.sum(cg * bg[s:s + 1, :], axis=-1, keepdims=True)
            w = jnp.where(rg >= s, jnp.exp(acg - acg[s:s + 1, :]), 0.0)
            yd = yd + cbs * w * xdtg[s:s + 1, :]
        xg = x[:, sl]
        y = (yd + y_off * eac[:, sl] + dsk_ref[:, sl] * xg) * _silu(z_ref[0][:, sl])
        y_ref[0, :, sl] = (_mean_sq_norm(y) * ng_ref[:, sl]).astype(y_ref.dtype)


def _ssd_sample(x_b, dte_b, bm_b, cm_b, z_b, a_e, dsk_e, norm_g, h0, *, groups, valid):
    b, rows, di = x_b.shape
    gn = bm_b.shape[2]
    n = h0.shape[2]
    assert valid + 3 <= rows
    seq = lambda i: (i, 0, 0)
    par = lambda i: (0, 0)
    return pl.pallas_call(
        functools.partial(_ssd_sample_body, groups=groups, valid=valid),
        grid=(b,),
        in_specs=[pl.BlockSpec((1, rows, di), seq),
                  pl.BlockSpec((1, rows, di), seq),
                  pl.BlockSpec((1, rows, gn), seq),
                  pl.BlockSpec((1, rows, gn), seq),
                  pl.BlockSpec((1, rows, di), seq),
                  pl.BlockSpec((1, di), par),
                  pl.BlockSpec((1, di), par),
                  pl.BlockSpec((1, di), par),
                  pl.BlockSpec((1, di, n), seq)],
        out_specs=[pl.BlockSpec((1, rows, di), seq),
                   pl.BlockSpec((1, di, n), seq)],
        out_shape=[jax.ShapeDtypeStruct((b, rows, di), F32),
                   jax.ShapeDtypeStruct((b, di, n), F32)],
        compiler_params=_cparams("arbitrary"),
        name="ssd_sample",
    )(x_b, dte_b, bm_b, cm_b, z_b, a_e, dsk_e, norm_g, h0)


def _layernorm(v, g, b):
    mu = jnp.mean(v, axis=-1, keepdims=True)
    d = v - mu
    var = jnp.mean(d * d, axis=-1, keepdims=True)
    return d * lax.rsqrt(var + EPS) * g + b


def _gmlp_prompt_body(gv_ref, gu_ref, lg_ref, lb_ref, ws_ref, bst_ref, o_ref):
    vn = _layernorm(gv_ref[0], lg_ref[...], lb_ref[...])
    cl, d = vn.shape
    groups = ws_ref.shape[0]
    gw = d // groups
    row = lax.broadcasted_iota(jnp.int32, (cl, cl), 0)
    col = lax.broadcasted_iota(jnp.int32, (cl, cl), 1)
    tril = row >= col
    for g in range(groups):
        sl = slice(g * gw, (g + 1) * gw)
        w = jnp.where(tril, ws_ref[g], 0.0).astype(BF16)
        s = jnp.dot(w, vn[:, sl].astype(BF16), preferred_element_type=F32) + bst_ref[:, g:g + 1]
        o_ref[0, :, sl] = (gu_ref[0][:, sl] * s).astype(o_ref.dtype)


def _gmlp_prompt(guv, ln_g, ln_b, ws, bs_t):
    b, ln, d2 = guv.shape
    d = d2 // 2
    cl = min(GMLP_CHUNK, ln)
    groups = ws.shape[0]
    return pl.pallas_call(
        _gmlp_prompt_body,
        grid=(b, ln // cl),
        in_specs=[pl.BlockSpec((1, cl, d), lambda i, c: (i, c, 1)),
                  pl.BlockSpec((1, cl, d), lambda i, c: (i, c, 0)),
                  pl.BlockSpec((1, d), lambda i, c: (0, 0)),
                  pl.BlockSpec((1, d), lambda i, c: (0, 0)),
                  pl.BlockSpec((groups, cl, cl), lambda i, c: (0, 0, 0)),
                  pl.BlockSpec((cl, groups), lambda i, c: (0, 0))],
        out_specs=pl.BlockSpec((1, cl, d), lambda i, c: (i, c, 0)),
        out_shape=jax.ShapeDtypeStruct((b, ln, d), BF16),
        compiler_params=_cparams("arbitrary", "arbitrary"),
        name="gmlp_prompt",
    )(guv, guv, ln_g, ln_b, ws, bs_t)


def _gmlp_slab_body(gv_ref, gu_ref, lg_ref, lb_ref, wrow_ref, brow_ref, o_ref, vn_ref):
    ln = gv_ref.shape[0]
    vn = [_layernorm(gv_ref[t], lg_ref[...], lb_ref[...]) for t in range(ln)]
    for t in range(ln):
        vn_ref[t] = vn[t]
        s = brow_ref[t:t + 1, :] + vn[0] * wrow_ref[t * ln:t * ln + 1, :]
        for k in range(1, t + 1):
            s = s + vn[k] * wrow_ref[t * ln + k:t * ln + k + 1, :]
        o_ref[t] = (gu_ref[t] * s).astype(o_ref.dtype)


def _gmlp_slab(guv, ln_g, ln_b, wrow, brow):
    ln, b, d2 = guv.shape
    d = d2 // 2
    bt = _row_tile(b, 32)
    return pl.pallas_call(
        _gmlp_slab_body,
        grid=(b // bt,),
        in_specs=[pl.BlockSpec((ln, bt, d), lambda i: (0, i, 1)),
                  pl.BlockSpec((ln, bt, d), lambda i: (0, i, 0)),
                  pl.BlockSpec((1, d), lambda i: (0, 0)),
                  pl.BlockSpec((1, d), lambda i: (0, 0)),
                  pl.BlockSpec((ln * ln, d), lambda i: (0, 0)),
                  pl.BlockSpec((ln, d), lambda i: (0, 0))],
        out_specs=[pl.BlockSpec((ln, bt, d), lambda i: (0, i, 0)),
                   pl.BlockSpec((ln, bt, d), lambda i: (0, i, 0))],
        out_shape=[jax.ShapeDtypeStruct((ln, b, d), BF16),
                   jax.ShapeDtypeStruct((ln, b, d), F32)],
        compiler_params=_cparams("arbitrary"),
        name="gmlp_sample",
    )(guv, guv, ln_g, ln_b, wrow, brow)


def _peer_scores_body(h_ref, wq_ref, k_ref, o_ref):
    q = jnp.dot(h_ref[...], wq_ref[...], preferred_element_type=F32).astype(BF16)
    half = k_ref.shape[3]
    for i in range(2):
        o_ref[0, i] = lax.dot_general(k_ref[0, i], q[:, i * half:(i + 1) * half], _NT,
                                      preferred_element_type=F32)


def _peer_scores(h2, wq, keys, tt):
    m, d = h2.shape
    heads, _, nk, half = keys.shape
    return pl.pallas_call(
        _peer_scores_body,
        grid=(heads, m // tt),
        in_specs=[pl.BlockSpec((tt, d), lambda h, t: (t, 0)),
                  pl.BlockSpec((d, 2 * half), lambda h, t: (0, h)),
                  pl.BlockSpec((1, 2, nk, half), lambda h, t: (h, 0, 0, 0))],
        out_specs=pl.BlockSpec((1, 2, nk, tt), lambda h, t: (h, 0, 0, t)),
        out_shape=jax.ShapeDtypeStruct((heads, 2, nk, m), F32),
        compiler_params=_cparams("arbitrary", "arbitrary"),
        name="peer_scores",
    )(h2, wq, keys)


def _top_values(s, k):
    n = s.shape[0]
    ridx = lax.broadcasted_iota(jnp.int32, s.shape, 0)
    kidx = lax.broadcasted_iota(jnp.int32, (k, s.shape[1]), 0)
    out = jnp.zeros((k, s.shape[1]), F32)
    for r in range(k):
        m = jnp.max(s, axis=0, keepdims=True)
        first = jnp.min(jnp.where(s == m, ridx, n), axis=0, keepdims=True)
        s = jnp.where(ridx == first, -jnp.inf, s)
        out = jnp.where(kidx == r, m, out)
    return out


def _pair_candidates(v1, v2, k):
    isplit = 4
    lanes = v1.shape[1]

    def rows(nr):
        return lax.broadcasted_iota(jnp.int32, (nr, lanes), 0)

    pieces = []
    for i in range(min(isplit, k)):
        nj = k // (i + 1)
        nr = -(-nj // SUBLANES) * SUBLANES
        p = v1[i:i + 1, :] + v2[0:nr, :]
        pieces.append(p if nr == nj else jnp.where(rows(nr) < nj, p, -jnp.inf))
    for j in range(k // (isplit + 1)):
        ni = k // (j + 1)
        nr = -(-ni // SUBLANES) * SUBLANES
        r = rows(nr)
        pieces.append(jnp.where((r >= isplit) & (r < ni), v1[0:nr, :] + v2[j:j + 1, :], -jnp.inf))
    return jnp.concatenate(pieces, axis=0)


def _peer_stats_body(s_ref, o_ref, *, topk):
    heads = s_ref.shape[0]
    for h in range(heads):
        v1 = _top_values(s_ref[h, 0], topk)
        v2 = _top_values(s_ref[h, 1], topk)
        top = _top_values(_pair_candidates(v1, v2, topk), topk)
        mx = top[0:1, :]
        lse = mx + jnp.log(jnp.sum(jnp.exp(top - mx), axis=0, keepdims=True))
        o_ref[h, 0:1, :] = top[topk - 1:topk, :]
        o_ref[h, 1:2, :] = lse


def _peer_stats(scores, tt):
    heads, _, nk, m = scores.shape
    return pl.pallas_call(
        functools.partial(_peer_stats_body, topk=PEER_TOPK),
        grid=(m // tt,),
        in_specs=[pl.BlockSpec((heads, 2, nk, tt), lambda t: (0, 0, 0, t))],
        out_specs=pl.BlockSpec((heads, 2, tt), lambda t: (0, 0, t)),
        out_shape=jax.ShapeDtypeStruct((heads, 2, m), F32),
        compiler_params=_cparams("arbitrary"),
        name="peer_stats",
    )(scores)


def _peer_mix_body(h_ref, u_ref, vt_ref, s_ref, st_ref, o_ref, *, nk, tc):
    e = pl.program_id(1)

    @pl.when(e == 0)
    def _():
        o_ref[...] = jnp.zeros_like(o_ref)

    et = u_ref.shape[0]
    heads = s_ref.shape[0]
    tt = h_ref.shape[0]
    for c0 in range(0, tt, tc):
        cs = slice(c0, c0 + tc)
        act = _gelu(lax.dot_general(u_ref[...], h_ref[cs, :], _NT, preferred_element_type=F32))
        parts = []
        for ii in range(et // nk):
            i = e * (et // nk) + ii
            wsum = jnp.zeros((nk, tc), F32)
            for h in range(heads):
                c = s_ref[h, 0, pl.ds(i, 1), cs] + s_ref[h, 1, :, cs]
                wsum = wsum + jnp.where(c >= st_ref[h, 0:1, cs], jnp.exp(c - st_ref[h, 1:2, cs]), 0.0)
            parts.append((wsum * act[ii * nk:(ii + 1) * nk, :]).astype(BF16))
        wg = jnp.concatenate(parts, axis=0) if len(parts) > 1 else parts[0]
        o_ref[:, cs] += jnp.dot(vt_ref[...], wg, preferred_element_type=F32)


def _peer_mix(h2, u, vt, scores, stats, tt, et):
    m, d = h2.shape
    ne = u.shape[0]
    heads, _, nk, _ = scores.shape
    return pl.pallas_call(
        functools.partial(_peer_mix_body, nk=nk, tc=_row_tile(tt, 256)),
        grid=(m // tt, ne // et),
        in_specs=[pl.BlockSpec((tt, d), lambda t, e: (t, 0)),
                  pl.BlockSpec((et, d), lambda t, e: (e, 0)),
                  pl.BlockSpec((d, et), lambda t, e: (0, e)),
                  pl.BlockSpec((heads, 2, nk, tt), lambda t, e: (0, 0, 0, t)),
                  pl.BlockSpec((heads, 2, tt), lambda t, e: (0, 0, t))],
        out_specs=pl.BlockSpec((d, tt), lambda t, e: (0, t)),
        out_shape=jax.ShapeDtypeStruct((d, m), F32),
        compiler_params=_cparams("arbitrary", "arbitrary"),
        name="peer_mix",
    )(h2, u, vt, scores, stats)


def _final_body(x_ref, pt_ref, gate_ref, fg_ref, o_ref):
    xn = x_ref[0] + gate_ref[0] * pt_ref[...].T
    o_ref[0] = _mean_sq_norm(xn) * fg_ref[...]


def _final(x3, peer_t, col0, mod, k_gate, final_g):
    gg, r, d = x3.shape
    tm = _row_tile(r, 256)
    nb = r // tm
    c0 = col0 // tm
    return pl.pallas_call(
        _final_body,
        grid=(gg, nb),
        in_specs=[pl.BlockSpec((1, tm, d), lambda b, i: (b, i, 0)),
                  pl.BlockSpec((d, tm), lambda b, i: (0, c0 + b * nb + i)),
                  _mod_spec(mod, k_gate, tm, d, 2),
                  pl.BlockSpec((1, d), lambda b, i: (0, 0))],
        out_specs=pl.BlockSpec((1, tm, d), lambda b, i: (b, i, 0)),
        out_shape=jax.ShapeDtypeStruct((gg, r, d), F32),
        compiler_params=_cparams("arbitrary", "arbitrary"),
        name="final",
    )(x3, peer_t, mod, final_g)


def _mixer(x3, mod, w, dims, *, sample, ssm0=None, conv0=None):
    d, d_inner, groups, d_state, headdim, conv_dim = dims
    heads = d_inner // headdim
    hpg = heads // groups
    h = _norm_mod(x3, w["norm1_g"], mod, 0, 1)
    z = _matmul(h, w["w_z"], _ep_none, name="in_z")
    xbc = _matmul(h, w["w_xbc"], _ep_none, name="in_xbc")
    dt = _matmul(h, w["w_dt"], _ep_softplus_bias, extras=[(w["dt_bias"], "bias")], name="in_dt")
    guv = _matmul(h, w["w_uv"], _ep_gelu, name="in_uv")
    sg = _matmul(h, w["w_g"], _ep_sigmoid, name="in_gates")

    if not sample:
        b, ln, _ = x3.shape
        xbc_act = _conv_prompt(xbc, w["conv_w"], w["conv_b"])
        conv_new = xbc[:, ln - (w["conv_w"].shape[0] - 1):, :]
        dt_g = jnp.pad(dt.reshape(b, ln, groups, hpg), ((0, 0), (0, 0), (0, 0), (0, LANES - hpg)))
        dt_g = dt_g.reshape(b, ln, groups * LANES)
        yn, h_new = _ssd_prompt(xbc_act, dt_g, z, w["alog_g"], w["dsk_e"], w["ssd_norm_g"],
                                d_inner=d_inner, groups=groups, d_state=d_state, headdim=headdim)
        ssm_new = h_new.reshape(b, heads, headdim, d_state)
        yb = _gmlp_prompt(guv, w["gmlp_ln_g"], w["gmlp_ln_b"], w["gmlp_ws"], w["gmlp_bs_t"])
        v_rows = None
    else:
        b = conv0.shape[0]
        ln = x3.shape[1] // b
        cw = w["conv_w"].shape[0]

        def slabs(t):
            return t.reshape(ln, b, t.shape[-1])

        def flat(t):
            return t.reshape(1, ln * b, t.shape[-1])

        xp = jnp.concatenate([jnp.transpose(conv0, (1, 0, 2)), slabs(xbc)], axis=0)
        xbc_act = _conv_slab(xp, w["conv_w"], w["conv_b"])
        conv_new = jnp.transpose(xp[ln:ln + cw - 1], (1, 0, 2))

        def to_rows(t):
            return jnp.pad(jnp.transpose(t, (1, 0, 2)), ((0, 0), (0, SUBLANES - ln), (0, 0)))

        act_b = to_rows(xbc_act)
        dte_b = jnp.repeat(to_rows(slabs(dt)), headdim, axis=2)
        yn_b, h_new = _ssd_sample(act_b[:, :, :d_inner], dte_b, act_b[:, :, d_inner:d_inner + groups * d_state],
                                  act_b[:, :, d_inner + groups * d_state:], to_rows(slabs(z)), w["a_e"], w["dsk_e"],
                                  w["ssd_norm_g"], ssm0.reshape(b, d_inner, d_state), groups=groups, valid=ln)
        ssm_new = h_new.reshape(b, heads, headdim, d_state)
        yn = flat(jnp.transpose(yn_b[:, :ln], (1, 0, 2)).astype(BF16))
        wrow, brow = _slab_gate_rows(w, ln)
        yb, v_tm = _gmlp_slab(slabs(guv), w["gmlp_ln_g"], w["gmlp_ln_b"], wrow, brow)
        yb = flat(yb)
        v_rows = jnp.transpose(v_tm, (1, 0, 2))

    y_a = _matmul(yn, w["w_oa"], _ep_none, name="out_a")
    merged = _matmul(yb, w["w_ob"], _ep_merge, extras=[(y_a, "tile"), (sg, "tile", 0), (sg, "tile", 1)],
                     out_dtype=BF16, name="out_b_merge")
    x1 = _matmul(merged, w["w_out"], _ep_residual, extras=[(x3, "tile"), (mod, "mod", 2)], name="out_proj")
    h2 = _norm_mod(x1, w["norm2_g"], mod, 3, 4)
    return x1, h2, ssm_new, conv_new, v_rows


def _slab_gate_rows(w, ln):
    ws, bs = w["gmlp_ws"], w["gmlp_bs"]
    gw = w["gmlp_ln_g"].shape[1] // ws.shape[0]
    wt = jnp.where(jnp.tril(jnp.ones((ln, ln), bool)), ws[:, :ln, :ln], 0.0)
    wrow = jnp.repeat(jnp.transpose(wt, (1, 2, 0)).reshape(ln * ln, -1), gw, axis=1)
    brow = jnp.repeat(jnp.transpose(bs[:, :ln]), gw, axis=1)
    return wrow, brow


def kernel(x_prompt, x_sample, state_ssm, state_conv, c_prompt, c_sample, w_ada, b_ada, norm1_g, w_in, conv_w, conv_b, dt_bias, a_log, d_skip, ssd_norm_g, w_oa, gmlp_ln_g, gmlp_ln_b, gmlp_ws, gmlp_bs, w_ob, w_out, norm2_g, peer_wq, peer_keys, peer_u, peer_v, final_g):
    depth = w_ada.shape[0]
    bp, lp, d = x_prompt.shape
    bs_, ls, _ = x_sample.shape
    heads = a_log.shape[1]
    headdim, d_state = state_ssm.shape[3], state_ssm.shape[4]
    d_inner = heads * headdim
    conv_dim = conv_w.shape[2]
    groups = (conv_dim - d_inner) // (2 * d_state)
    hpg = heads // groups
    d_gmlp = gmlp_ln_g.shape[1]
    dims = (d, d_inner, groups, d_state, headdim, conv_dim)
    off_xbc = d_inner
    off_dt = off_xbc + conv_dim
    off_u = off_dt + heads
    off_ga = off_u + 2 * d_gmlp

    xp = x_prompt
    xs = jnp.transpose(x_sample, (1, 0, 2)).reshape(1, ls * bs_, d)
    n_c = bs_ + bp
    c_all = jnp.pad(jnp.concatenate([c_sample, c_prompt], axis=0), ((0, (-n_c) % SUBLANES), (0, 0)))
    ssm_p, conv_p, ssm_s, conv_s, v_s = [], [], [], [], []
    for l in range(depth):
        mod = _ada(c_all, w_ada[l], b_ada[l][None, :])
        mod_s = jnp.tile(mod[:bs_], (ls, 1))[None]
        mod_p = mod[bs_:n_c][:, None, :]
        wl = w_in[l]
        a = -jnp.exp(a_log[l])
        w = dict(
            norm1_g=norm1_g[l][None, :], norm2_g=norm2_g[l][None, :],
            w_z=wl[:, :off_xbc].astype(BF16), w_xbc=wl[:, off_xbc:off_dt].astype(BF16),
            w_dt=wl[:, off_dt:off_u].astype(BF16), w_uv=wl[:, off_u:off_ga].astype(BF16),
            w_g=wl[:, off_ga:].astype(BF16), dt_bias=dt_bias[l][None, :],
            conv_w=conv_w[l], conv_b=conv_b[l][None, :],
            alog_g=jnp.pad(a_log[l].reshape(groups, hpg), ((0, 0), (0, LANES - hpg))).reshape(1, groups * LANES),
            a_e=jnp.repeat(a, headdim)[None, :], dsk_e=jnp.repeat(d_skip[l], headdim)[None, :],
            ssd_norm_g=ssd_norm_g[l][None, :],
            w_oa=w_oa[l].astype(BF16), w_ob=w_ob[l].astype(BF16), w_out=w_out[l].astype(BF16),
            gmlp_ln_g=gmlp_ln_g[l][None, :], gmlp_ln_b=gmlp_ln_b[l][None, :],
            gmlp_ws=gmlp_ws[l], gmlp_bs=gmlp_bs[l], gmlp_bs_t=jnp.transpose(gmlp_bs[l]),
        )
        x1p, h2p, hp, cp, _ = _mixer(xp, mod_p, w, dims, sample=False)
        x1s, h2s, hs, cs, vs = _mixer(xs, mod_s, w, dims, sample=True, ssm0=state_ssm[l], conv0=state_conv[l])

        h2 = jnp.concatenate([h2p.reshape(bp * lp, d), h2s.reshape(ls * bs_, d)], axis=0)
        m = h2.shape[0]
        tt = _row_tile(m, 512)
        ne = peer_u.shape[1]
        et = _row_tile(ne, 512)
        scores = _peer_scores(h2, peer_wq[l].astype(BF16), peer_keys[l].astype(BF16), tt)
        stats = _peer_stats(scores, tt)
        peer_t = _peer_mix(h2, peer_u[l].astype(BF16), jnp.transpose(peer_v[l]).astype(BF16), scores, stats, tt, et)
        last = l == depth - 1
        fg = final_g[None, :] if last else jnp.ones((1, d), F32)
        assert last, "only DEPTH == 1 is supported"
        xp = _final(x1p, peer_t, 0, mod_p, 5, fg)
        xs = _final(x1s, peer_t, bp * lp, mod_s, 5, fg)
        ssm_p.append(hp)
        conv_p.append(cp)
        ssm_s.append(hs)
        conv_s.append(cs)
        v_s.append(vs)
    y_prompt = xp
    y_sample = jnp.transpose(xs.reshape(ls, bs_, d), (1, 0, 2))
    return (y_prompt, y_sample, jnp.stack(ssm_p), jnp.stack(conv_p), jnp.stack(ssm_s), jnp.stack(conv_s),
            jnp.stack(v_s))
```

```python
import functools

import jax
import jax.numpy as jnp
from jax import lax
from jax.experimental import pallas as pl
from jax.experimental.pallas import tpu as pltpu

F32 = jnp.float32
BF16 = jnp.bfloat16
EPS = 1e-6
N_MOD = 6
PEER_TOPK = 16
SSD_CHUNK = 128
GMLP_CHUNK = 128
SUBLANES = 8
LANES = 128
VMEM_LIMIT = 56 * 1024 * 1024
MATMUL_VMEM_BUDGET = 40 * 1024 * 1024
_SQRT_HALF = 0.7071067811865476
_HIGHEST = lax.Precision.HIGHEST
_NT = (((1,), (1,)), ((), ()))
_TN = (((0,), (0,)), ((), ()))


def _cparams(*sem):
    return pltpu.CompilerParams(dimension_semantics=sem, vmem_limit_bytes=VMEM_LIMIT)


def _silu(x):
    return x * jax.nn.sigmoid(x)


def _gelu(x):
    return 0.5 * x * (1.0 + lax.erf(x * _SQRT_HALF))


def _softplus(x):
    return jnp.maximum(x, 0.0) + jnp.log1p(jnp.exp(-jnp.abs(x)))


def _mean_sq_norm(x):
    return x * lax.rsqrt(jnp.mean(x * x, axis=-1, keepdims=True) + EPS)


def _ada_body(c_ref, w_ref, b_ref, o_ref):
    a = _silu(c_ref[...]).astype(BF16)
    o_ref[...] = jnp.dot(a, w_ref[...].astype(BF16), preferred_element_type=F32) + b_ref[...]


def _ada(c_all, w_ada, b_ada):
    m, d = c_all.shape
    n = w_ada.shape[1]
    tn = min(512, n)
    return pl.pallas_call(
        _ada_body,
        grid=(n // tn,),
        in_specs=[pl.BlockSpec((m, d), lambda j: (0, 0)),
                  pl.BlockSpec((d, tn), lambda j: (0, j)),
                  pl.BlockSpec((1, tn), lambda j: (0, j))],
        out_specs=pl.BlockSpec((m, tn), lambda j: (0, j)),
        out_shape=jax.ShapeDtypeStruct((m, n), F32),
        compiler_params=_cparams("arbitrary"),
        name="ada",
    )(c_all, w_ada, b_ada)


def _mod_block(mod, k, tm, tn):
    nblk = mod.shape[2] // N_MOD // tn
    if mod.shape[1] == 1:
        return (1, 1, tn), lambda j, g, i: (g, 0, k * nblk + j)
    return (1, tm, tn), lambda j, g, i: (0, i, k * nblk + j)


def _mod_spec(mod, k, tm, d):
    shape, index = _mod_block(mod, k, tm, d)
    return pl.BlockSpec(shape, lambda g, i: index(0, g, i))


def _row_tile(r, target):
    t = min(r, target)
    while r % t:
        t //= 2
    return t


def _norm_mod_body(x_ref, g_ref, sc_ref, sh_ref, o_ref):
    y = _mean_sq_norm(x_ref[0]) * g_ref[...]
    o_ref[0] = (y * (1.0 + sc_ref[0]) + sh_ref[0]).astype(o_ref.dtype)


def _norm_mod(x3, g, mod, k_shift, k_scale):
    gg, r, d = x3.shape
    tm = _row_tile(r, 512)
    return pl.pallas_call(
        _norm_mod_body,
        grid=(gg, r // tm),
        in_specs=[pl.BlockSpec((1, tm, d), lambda b, i: (b, i, 0)),
                  pl.BlockSpec((1, d), lambda b, i: (0, 0)),
                  _mod_spec(mod, k_scale, tm, d),
                  _mod_spec(mod, k_shift, tm, d)],
        out_specs=pl.BlockSpec((1, tm, d), lambda b, i: (b, i, 0)),
        out_shape=jax.ShapeDtypeStruct((gg, r, d), BF16),
        compiler_params=_cparams("arbitrary", "arbitrary"),
        name="norm_mod",
    )(x3, g, mod, mod)


def _mm_body(*refs, epilogue, kinds):
    a_ref, w_ref = refs[0], refs[1]
    o_ref = refs[-1]
    acc = jnp.dot(a_ref[0], w_ref[...], preferred_element_type=F32)
    ex = [r[...] if kd == "bias" else r[0] for r, kd in zip(refs[2:-1], kinds)]
    o_ref[0] = epilogue(acc, *ex).astype(o_ref.dtype)


def _matmul(a3, w, epilogue, extras=(), out_dtype=F32, lhs_resident=False, name="mm"):
    gg, r, k = a3.shape
    n = w.shape[1]
    tm = _row_tile(r, 1024)
    tn = _row_tile(n, 1024)
    tile_bytes = sum(e[0].dtype.itemsize for e in extras if e[1] == "tile") + jnp.dtype(out_dtype).itemsize

    def vmem_bytes(tm_, tn_):
        return 2 * (tm_ * k * 2 + k * tn_ * 2 + tm_ * tn_ * tile_bytes) + 2 * tm_ * tn_ * 4

    while vmem_bytes(tm, tn) > MATMUL_VMEM_BUDGET and (tm > 256 or tn > 256):
        shrink_rows = tm > tn if lhs_resident else tm >= tn
        if (shrink_rows and tm > 256) or tn <= 256:
            tm //= 2
        else:
            tn //= 2

    def spec(shape, index):
        if lhs_resident:
            return pl.BlockSpec(shape, lambda g, i, j: index(j, g, i))
        return pl.BlockSpec(shape, index)

    in_specs = [spec((1, tm, k), lambda j, g, i: (g, i, 0)),
                spec((k, tn), lambda j, g, i: (0, j))]
    args = [a3, w]
    kinds = []
    for e in extras:
        arr, kind = e[0], e[1]
        kinds.append(kind)
        args.append(arr)
        if kind == "tile":
            off = e[2] if len(e) > 2 else 0
            nb = n // tn
            in_specs.append(spec((1, tm, tn), lambda j, g, i, off=off, nb=nb: (g, i, off * nb + j)))
        elif kind == "bias":
            in_specs.append(spec((1, tn), lambda j, g, i: (0, j)))
        else:
            in_specs.append(spec(*_mod_block(arr, e[2], tm, tn)))
    return pl.pallas_call(
        functools.partial(_mm_body, epilogue=epilogue, kinds=tuple(kinds)),
        grid=(gg, r // tm, n // tn) if lhs_resident else (n // tn, gg, r // tm),
        in_specs=in_specs,
        out_specs=spec((1, tm, tn), lambda j, g, i: (g, i, j)),
        out_shape=jax.ShapeDtypeStruct((gg, r, n), out_dtype),
        compiler_params=_cparams("arbitrary", "arbitrary", "arbitrary"),
        name=name,
    )(*args)


def _ep_none(acc):
    return acc


def _ep_gelu(acc):
    return _gelu(acc)


def _ep_sigmoid(acc):
    return jax.nn.sigmoid(acc)


def _ep_softplus_bias(acc, bias):
    return _softplus(acc + bias)


def _ep_merge(acc, y_a, sg_a, sg_b):
    return sg_a * y_a + sg_b * acc


def _ep_residual(acc, x, gate):
    return x + gate * acc


def _conv_prompt_body(x_ref, w_ref, b_ref, o_ref, pad_ref, *, width):
    ln = x_ref.shape[1]
    pad_ref[0:SUBLANES, :] = jnp.zeros((SUBLANES, pad_ref.shape[1]), F32)
    pad_ref[SUBLANES:SUBLANES + ln, :] = x_ref[0]
    acc = b_ref[...] + x_ref[0] * w_ref[width - 1:width, :]
    for k in range(width - 1):
        s = SUBLANES - (width - 1) + k
        acc = acc + pad_ref[s:s + ln, :] * w_ref[k:k + 1, :]
    o_ref[0] = _silu(acc)


def _conv_prompt(x3, conv_w, conv_b):
    b, ln, c = x3.shape
    width = conv_w.shape[0]
    ct = _row_tile(c, 512)
    return pl.pallas_call(
        functools.partial(_conv_prompt_body, width=width),
        grid=(b, c // ct),
        in_specs=[pl.BlockSpec((1, ln, ct), lambda i, j: (i, 0, j)),
                  pl.BlockSpec((width, ct), lambda i, j: (0, j)),
                  pl.BlockSpec((1, ct), lambda i, j: (0, j))],
        out_specs=pl.BlockSpec((1, ln, ct), lambda i, j: (i, 0, j)),
        out_shape=jax.ShapeDtypeStruct((b, ln, c), F32),
        scratch_shapes=[pltpu.VMEM((ln + SUBLANES, ct), F32)],
        compiler_params=_cparams("arbitrary", "arbitrary"),
        name="conv_prompt",
    )(x3, conv_w, conv_b)


def _conv_slab_body(xp_ref, w_ref, b_ref, o_ref, *, width):
    for t in range(o_ref.shape[0]):
        acc = b_ref[...] + xp_ref[t] * w_ref[0:1, :]
        for k in range(1, width):
            acc = acc + xp_ref[t + k] * w_ref[k:k + 1, :]
        o_ref[t] = _silu(acc)


def _conv_slab(xp, conv_w, conv_b):
    lp, b, c = xp.shape
    width = conv_w.shape[0]
    ln = lp - width + 1
    ct = _row_tile(c, 1024)
    return pl.pallas_call(
        functools.partial(_conv_slab_body, width=width),
        grid=(c // ct,),
        in_specs=[pl.BlockSpec((lp, b, ct), lambda j: (0, 0, j)),
                  pl.BlockSpec((width, ct), lambda j: (0, j)),
                  pl.BlockSpec((1, ct), lambda j: (0, j))],
        out_specs=pl.BlockSpec((ln, b, ct), lambda j: (0, 0, j)),
        out_shape=jax.ShapeDtypeStruct((ln, b, c), F32),
        compiler_params=_cparams("arbitrary"),
        name="conv_sample",
    )(xp, conv_w, conv_b)


def _ssd_prompt_body(x_ref, b_ref, c_ref, dt_ref, z_ref, alog_ref, dsk_ref, ng_ref, y_ref, hl_ref,
                     h_scr, y_scr, *, headdim):
    ci = pl.program_id(2)

    @pl.when(ci == 0)
    def _():
        h_scr[...] = jnp.zeros_like(h_scr)

    x = x_ref[0]
    cl, gw = x.shape
    dt = dt_ref[0]
    a = -jnp.exp(alog_ref[...])
    row = lax.broadcasted_iota(jnp.int32, (cl, cl), 0)
    col = lax.broadcasted_iota(jnp.int32, (cl, cl), 1)
    tril = row >= col
    acum = jnp.dot(tril.astype(F32), dt * a, precision=_HIGHEST, preferred_element_type=F32)
    acum_t = acum.T
    lane_head = lax.broadcasted_iota(jnp.int32, (cl, LANES), 1) // headdim
    heads_per_blk = LANES // headdim
    bb = b_ref[0].astype(BF16)
    cc = c_ref[0].astype(BF16)
    cb = lax.dot_general(cc, bb, _NT, preferred_element_type=F32)
    h_t = h_scr[...]
    y_off = jnp.dot(cc, h_t.astype(BF16), preferred_element_type=F32)
    for j in range(gw // LANES):
        sl = slice(j * LANES, (j + 1) * LANES)
        h0 = j * heads_per_blk
        dte = jnp.broadcast_to(dt[:, h0:h0 + 1], (cl, LANES))
        ace = jnp.broadcast_to(acum[:, h0:h0 + 1], (cl, LANES))
        for q in range(1, heads_per_blk):
            dte = jnp.where(lane_head == q, dt[:, h0 + q:h0 + q + 1], dte)
            ace = jnp.where(lane_head == q, acum[:, h0 + q:h0 + q + 1], ace)
        xj = x[:, sl]
        xdt = xj * dte
        last = ace[cl - 1:cl, :]
        yd = jnp.zeros((cl, LANES), F32)
        for q in range(heads_per_blk):
            hh = h0 + q
            seg = acum[:, hh:hh + 1] - acum_t[hh:hh + 1, :]
            m = (cb * jnp.exp(jnp.where(tril, seg, -jnp.inf))).astype(BF16)
            xm = jnp.where(lane_head == q, xdt, 0.0).astype(BF16)
            yd = yd + jnp.dot(m, xm, preferred_element_type=F32)
        y = yd + y_off[:, sl] * jnp.exp(ace) + dsk_ref[:, sl] * xj
        y_scr[:, sl] = y * _silu(z_ref[0][:, sl])
        st_t = lax.dot_general(bb, (xdt * jnp.exp(last - ace)).astype(BF16), _TN,
                               preferred_element_type=F32)
        h_scr[:, sl] = h_t[:, sl] * jnp.exp(last) + st_t
    y_ref[0] = (_mean_sq_norm(y_scr[...]) * ng_ref[...]).astype(y_ref.dtype)

    @pl.when(ci == pl.num_programs(2) - 1)
    def _():
        hl_ref[0] = h_scr[...].T


def _ssd_prompt(xbc_act, dt_g, z, alog_g, dsk_e, norm_g, *, d_inner, groups, d_state, headdim):
    b, ln, _ = xbc_act.shape
    gw = d_inner // groups
    cl = min(SSD_CHUNK, ln)
    nc = ln // cl
    xb = d_inner // d_state
    return pl.pallas_call(
        functools.partial(_ssd_prompt_body, headdim=headdim),
        grid=(b, groups, nc),
        in_specs=[pl.BlockSpec((1, cl, gw), lambda i, g, c: (i, c, g)),
                  pl.BlockSpec((1, cl, d_state), lambda i, g, c: (i, c, xb + g)),
                  pl.BlockSpec((1, cl, d_state), lambda i, g, c: (i, c, xb + groups + g)),
                  pl.BlockSpec((1, cl, LANES), lambda i, g, c: (i, c, g)),
                  pl.BlockSpec((1, cl, gw), lambda i, g, c: (i, c, g)),
                  pl.BlockSpec((1, LANES), lambda i, g, c: (0, g)),
                  pl.BlockSpec((1, gw), lambda i, g, c: (0, g)),
                  pl.BlockSpec((1, gw), lambda i, g, c: (0, g))],
        out_specs=[pl.BlockSpec((1, cl, gw), lambda i, g, c: (i, c, g)),
                   pl.BlockSpec((1, gw, d_state), lambda i, g, c: (i, g, 0))],
        out_shape=[jax.ShapeDtypeStruct((b, ln, d_inner), BF16),
                   jax.ShapeDtypeStruct((b, d_inner, d_state), F32)],
        scratch_shapes=[pltpu.VMEM((d_state, gw), F32), pltpu.VMEM((cl, gw), F32)],
        compiler_params=_cparams("arbitrary", "arbitrary", "arbitrary"),
        name="ssd_prompt",
    )(xbc_act, xbc_act, xbc_act, dt_g, z, alog_g, dsk_e, norm_g)


def _ssd_sample_body(x_ref, dte_ref, b_ref, c_ref, z_ref, ae_ref, dsk_ref, ng_ref, h0_ref, y_ref, hn_ref,
                     *, groups, valid):
    x = x_ref[0]
    rows, di = x.shape
    gw = di // groups
    n = h0_ref.shape[2]
    rix = lax.broadcasted_iota(jnp.int32, (rows, di), 0)
    dte = jnp.where(rix < valid, dte_ref[0], 0.0)
    ac = dte * ae_ref[...]
    for sh in (1, 2, 4):
        ac = ac + jnp.where(rix >= sh, pltpu.roll(ac, sh, axis=0), 0.0)
    xdt = x * dte
    last = ac[rows - 1:rows, :]
    cdec = jnp.exp(last)
    hi = cdec.astype(BF16)
    r1 = cdec - hi.astype(F32)
    mid = r1.astype(BF16)
    lo = (r1 - mid.astype(F32)).astype(BF16)
    lhs = (xdt * jnp.exp(last - ac)).astype(BF16)
    lhs = jnp.where(rix == valid, hi, jnp.where(rix == valid + 1, mid, jnp.where(rix == valid + 2, lo, lhs)))
    rn = lax.broadcasted_iota(jnp.int32, (rows, n), 0)
    ones_rows = jnp.where((rn >= valid) & (rn < valid + 3), 1.0, 0.0).astype(BF16)
    rg = lax.broadcasted_iota(jnp.int32, (rows, gw), 0)
    eac = jnp.exp(ac)
    for g in range(groups):
        sl = slice(g * gw, (g + 1) * gw)
        bg = b_ref[0][:, g * n:(g + 1) * n]
        cg = c_ref[0][:, g * n:(g + 1) * n]
        h0 = h0_ref[0, sl, :]
        y_off = lax.dot_general(cg.astype(BF16), h0.astype(BF16), _NT, preferred_element_type=F32)
        rhs = jnp.concatenate([jnp.where(rn < valid, bg, 0.0).astype(BF16), ones_rows], axis=1)
        res = lax.dot_general(lhs[:, sl], rhs, _TN, preferred_element_type=F32)
        hn_ref[0, sl, :] = h0 * res[:, n:] + res[:, :n]
        acg = ac[:, sl]
        xdtg = xdt[:, sl]
        yd = jnp.zeros((rows, gw), F32)
        for s in range(valid):
            cbs = jnp.sum(cg * bg[s:s + 1, :], axis=-1, keepdims=True)
            w = jnp.where(rg >= s, jnp.exp(acg - acg[s:s + 1, :]), 0.0)
            yd = yd + cbs * w * xdtg[s:s + 1, :]
        xg = x[:, sl]
        y = (yd + y_off * eac[:, sl] + dsk_ref[:, sl] * xg) * _silu(z_ref[0][:, sl])
        y_ref[0, :, sl] = (_mean_sq_norm(y) * ng_ref[:, sl]).astype(y_ref.dtype)


def _ssd_sample(x_b, dte_b, bm_b, cm_b, z_b, a_e, dsk_e, norm_g, h0, *, groups, valid):
    b, rows, di = x_b.shape
    gn = bm_b.shape[2]
    n = h0.shape[2]
    assert valid + 3 <= rows
    seq = lambda i: (i, 0, 0)
    par = lambda i: (0, 0)
    return pl.pallas_call(
        functools.partial(_ssd_sample_body, groups=groups, valid=valid),
        grid=(b,),
        in_specs=[pl.BlockSpec((1, rows, di), seq),
                  pl.BlockSpec((1, rows, di), seq),
                  pl.BlockSpec((1, rows, gn), seq),
                  pl.BlockSpec((1, rows, gn), seq),
                  pl.BlockSpec((1, rows, di), seq),
                  pl.BlockSpec((1, di), par),
                  pl.BlockSpec((1, di), par),
                  pl.BlockSpec((1, di), par),
                  pl.BlockSpec((1, di, n), seq)],
        out_specs=[pl.BlockSpec((1, rows, di), seq),
                   pl.BlockSpec((1, di, n), seq)],
        out_shape=[jax.ShapeDtypeStruct((b, rows, di), F32),
                   jax.ShapeDtypeStruct((b, di, n), F32)],
        compiler_params=_cparams("arbitrary"),
        name="ssd_sample",
    )(x_b, dte_b, bm_b, cm_b, z_b, a_e, dsk_e, norm_g, h0)


def _layernorm(v, g, b):
    mu = jnp.mean(v, axis=-1, keepdims=True)
    d = v - mu
    var = jnp.mean(d * d, axis=-1, keepdims=True)
    return d * lax.rsqrt(var + EPS) * g + b


def _gmlp_prompt_body(gv_ref, gu_ref, lg_ref, lb_ref, ws_ref, bst_ref, o_ref):
    vn = _layernorm(gv_ref[0], lg_ref[...], lb_ref[...])
    cl, d = vn.shape
    groups = ws_ref.shape[0]
    gw = d // groups
    row = lax.broadcasted_iota(jnp.int32, (cl, cl), 0)
    col = lax.broadcasted_iota(jnp.int32, (cl, cl), 1)
    tril = row >= col
    for g in range(groups):
        sl = slice(g * gw, (g + 1) * gw)
        w = jnp.where(tril, ws_ref[g], 0.0).astype(BF16)
        s = jnp.dot(w, vn[:, sl].astype(BF16), preferred_element_type=F32) + bst_ref[:, g:g + 1]
        o_ref[0, :, sl] = (gu_ref[0][:, sl] * s).astype(o_ref.dtype)


def _gmlp_prompt(guv, ln_g, ln_b, ws, bs_t):
    b, ln, d2 = guv.shape
    d = d2 // 2
    cl = min(GMLP_CHUNK, ln)
    groups = ws.shape[0]
    return pl.pallas_call(
        _gmlp_prompt_body,
        grid=(b, ln // cl),
        in_specs=[pl.BlockSpec((1, cl, d), lambda i, c: (i, c, 1)),
                  pl.BlockSpec((1, cl, d), lambda i, c: (i, c, 0)),
                  pl.BlockSpec((1, d), lambda i, c: (0, 0)),
                  pl.BlockSpec((1, d), lambda i, c: (0, 0)),
                  pl.BlockSpec((groups, cl, cl), lambda i, c: (0, 0, 0)),
                  pl.BlockSpec((cl, groups), lambda i, c: (0, 0))],
        out_specs=pl.BlockSpec((1, cl, d), lambda i, c: (i, c, 0)),
        out_shape=jax.ShapeDtypeStruct((b, ln, d), BF16),
        compiler_params=_cparams("arbitrary", "arbitrary"),
        name="gmlp_prompt",
    )(guv, guv, ln_g, ln_b, ws, bs_t)


def _gmlp_slab_body(gv_ref, gu_ref, lg_ref, lb_ref, wrow_ref, brow_ref, o_ref, vn_ref):
    ln = gv_ref.shape[0]
    vn = [_layernorm(gv_ref[t], lg_ref[...], lb_ref[...]) for t in range(ln)]
    for t in range(ln):
        vn_ref[t] = vn[t]
        s = brow_ref[t:t + 1, :] + vn[0] * wrow_ref[t * ln:t * ln + 1, :]
        for k in range(1, t + 1):
            s = s + vn[k] * wrow_ref[t * ln + k:t * ln + k + 1, :]
        o_ref[t] = (gu_ref[t] * s).astype(o_ref.dtype)


def _gmlp_slab(guv, ln_g, ln_b, wrow, brow):
    ln, b, d2 = guv.shape
    d = d2 // 2
    bt = _row_tile(b, 32)
    return pl.pallas_call(
        _gmlp_slab_body,
        grid=(b // bt,),
        in_specs=[pl.BlockSpec((ln, bt, d), lambda i: (0, i, 1)),
                  pl.BlockSpec((ln, bt, d), lambda i: (0, i, 0)),
                  pl.BlockSpec((1, d), lambda i: (0, 0)),
                  pl.BlockSpec((1, d), lambda i: (0, 0)),
                  pl.BlockSpec((ln * ln, d), lambda i: (0, 0)),
                  pl.BlockSpec((ln, d), lambda i: (0, 0))],
        out_specs=[pl.BlockSpec((ln, bt, d), lambda i: (0, i, 0)),
                   pl.BlockSpec((ln, bt, d), lambda i: (0, i, 0))],
        out_shape=[jax.ShapeDtypeStruct((ln, b, d), BF16),
                   jax.ShapeDtypeStruct((ln, b, d), F32)],
        compiler_params=_cparams("arbitrary"),
        name="gmlp_sample",
    )(guv, guv, ln_g, ln_b, wrow, brow)


def _peer_scores_body(h_ref, wq_ref, k_ref, o_ref):
    q = jnp.dot(h_ref[...], wq_ref[...], preferred_element_type=F32).astype(BF16)
    half = k_ref.shape[3]
    for i in range(2):
        o_ref[0, i] = lax.dot_general(k_ref[0, i], q[:, i * half:(i + 1) * half], _NT,
                                      preferred_element_type=F32)


def _peer_scores(h2, wq, keys, tt):
    m, d = h2.shape
    heads, _, nk, half = keys.shape
    return pl.pallas_call(
        _peer_scores_body,
        grid=(heads, m // tt),
        in_specs=[pl.BlockSpec((tt, d), lambda h, t: (t, 0)),
                  pl.BlockSpec((d, 2 * half), lambda h, t: (0, h)),
                  pl.BlockSpec((1, 2, nk, half), lambda h, t: (h, 0, 0, 0))],
        out_specs=pl.BlockSpec((1, 2, nk, tt), lambda h, t: (h, 0, 0, t)),
        out_shape=jax.ShapeDtypeStruct((heads, 2, nk, m), F32),
        compiler_params=_cparams("arbitrary", "arbitrary"),
        name="peer_scores",
    )(h2, wq, keys)


def _top_values(s, k):
    n = s.shape[0]
    ridx = lax.broadcasted_iota(jnp.int32, s.shape, 0)
    kidx = lax.broadcasted_iota(jnp.int32, (k, s.shape[1]), 0)
    out = jnp.zeros((k, s.shape[1]), F32)
    for r in range(k):
        m = jnp.max(s, axis=0, keepdims=True)
        first = jnp.min(jnp.where(s == m, ridx, n), axis=0, keepdims=True)
        s = jnp.where(ridx == first, -jnp.inf, s)
        out = jnp.where(kidx == r, m, out)
    return out


def _pair_candidates(v1, v2, k):
    isplit = 4
    lanes = v1.shape[1]

    def rows(nr):
        return lax.broadcasted_iota(jnp.int32, (nr, lanes), 0)

    pieces = []
    for i in range(min(isplit, k)):
        nj = k // (i + 1)
        nr = -(-nj // SUBLANES) * SUBLANES
        p = v1[i:i + 1, :] + v2[0:nr, :]
        pieces.append(p if nr == nj else jnp.where(rows(nr) < nj, p, -jnp.inf))
    for j in range(k // (isplit + 1)):
        ni = k // (j + 1)
        nr = -(-ni // SUBLANES) * SUBLANES
        r = rows(nr)
        pieces.append(jnp.where((r >= isplit) & (r < ni), v1[0:nr, :] + v2[j:j + 1, :], -jnp.inf))
    return jnp.concatenate(pieces, axis=0)


def _peer_stats_body(s_ref, o_ref, *, topk):
    heads = s_ref.shape[0]
    for h in range(heads):
        v1 = _top_values(s_ref[h, 0], topk)
        v2 = _top_values(s_ref[h, 1], topk)
        top = _top_values(_pair_candidates(v1, v2, topk), topk)
        mx = top[0:1, :]
        lse = mx + jnp.log(jnp.sum(jnp.exp(top - mx), axis=0, keepdims=True))
        o_ref[h, 0:1, :] = top[topk - 1:topk, :]
        o_ref[h, 1:2, :] = lse


def _peer_stats(scores, tt):
    heads, _, nk, m = scores.shape
    return pl.pallas_call(
        functools.partial(_peer_stats_body, topk=PEER_TOPK),
        grid=(m // tt,),
        in_specs=[pl.BlockSpec((heads, 2, nk, tt), lambda t: (0, 0, 0, t))],
        out_specs=pl.BlockSpec((heads, 2, tt), lambda t: (0, 0, t)),
        out_shape=jax.ShapeDtypeStruct((heads, 2, m), F32),
        compiler_params=_cparams("arbitrary"),
        name="peer_stats",
    )(scores)


def _peer_mix_body(ht_ref, u_ref, vt_ref, s_ref, st_ref, o_ref, *, nk):
    e = pl.program_id(1)

    @pl.when(e == 0)
    def _():
        o_ref[...] = jnp.zeros_like(o_ref)

    et = u_ref.shape[0]
    heads = s_ref.shape[0]
    act = _gelu(jnp.dot(u_ref[...], ht_ref[...], preferred_element_type=F32))
    parts = []
    for ii in range(et // nk):
        i = e * (et // nk) + ii
        wsum = jnp.zeros((nk, act.shape[1]), F32)
        for h in range(heads):
            c = s_ref[h, 0, pl.ds(i, 1), :] + s_ref[h, 1]
            wsum = wsum + jnp.where(c >= st_ref[h, 0:1, :], jnp.exp(c - st_ref[h, 1:2, :]), 0.0)
        parts.append((wsum * act[ii * nk:(ii + 1) * nk, :]).astype(BF16))
    wg = jnp.concatenate(parts, axis=0) if len(parts) > 1 else parts[0]
    o_ref[...] += jnp.dot(vt_ref[...], wg, preferred_element_type=F32)


def _peer_mix(h2t, u, vt, scores, stats, tt, et):
    d, m = h2t.shape
    ne = u.shape[0]
    heads, _, nk, _ = scores.shape
    return pl.pallas_call(
        functools.partial(_peer_mix_body, nk=nk),
        grid=(m // tt, ne // et),
        in_specs=[pl.BlockSpec((d, tt), lambda t, e: (0, t)),
                  pl.BlockSpec((et, d), lambda t, e: (e, 0)),
                  pl.BlockSpec((d, et), lambda t, e: (0, e)),
                  pl.BlockSpec((heads, 2, nk, tt), lambda t, e: (0, 0, 0, t)),
                  pl.BlockSpec((heads, 2, tt), lambda t, e: (0, 0, t))],
        out_specs=pl.BlockSpec((d, tt), lambda t, e: (0, t)),
        out_shape=jax.ShapeDtypeStruct((d, m), F32),
        compiler_params=_cparams("arbitrary", "arbitrary"),
        name="peer_mix",
    )(h2t, u, vt, scores, stats)


def _final_body(x_ref, pt_ref, gate_ref, fg_ref, o_ref):
    xn = x_ref[0] + gate_ref[0] * pt_ref[...].T
    o_ref[0] = _mean_sq_norm(xn) * fg_ref[...]


def _final(x3, peer_t, col0, mod, k_gate, final_g):
    gg, r, d = x3.shape
    tm = _row_tile(r, 256)
    nb = r // tm
    c0 = col0 // tm
    return pl.pallas_call(
        _final_body,
        grid=(gg, nb),
        in_specs=[pl.BlockSpec((1, tm, d), lambda b, i: (b, i, 0)),
                  pl.BlockSpec((d, tm), lambda b, i: (0, c0 + b * nb + i)),
                  _mod_spec(mod, k_gate, tm, d),
                  pl.BlockSpec((1, d), lambda b, i: (0, 0))],
        out_specs=pl.BlockSpec((1, tm, d), lambda b, i: (b, i, 0)),
        out_shape=jax.ShapeDtypeStruct((gg, r, d), F32),
        compiler_params=_cparams("arbitrary", "arbitrary"),
        name="final",
    )(x3, peer_t, mod, final_g)


def _mixer(x3, mod, w, dims, *, sample, ssm0=None, conv0=None):
    d, d_inner, groups, d_state, headdim, conv_dim = dims
    heads = d_inner // headdim
    hpg = heads // groups
    h = _norm_mod(x3, w["norm1_g"], mod, 0, 1)
    z = _matmul(h, w["w_z"], _ep_none, lhs_resident=True, name="in_z")
    xbc = _matmul(h, w["w_xbc"], _ep_none, name="in_xbc")
    dt = _matmul(h, w["w_dt"], _ep_softplus_bias, extras=[(w["dt_bias"], "bias")], name="in_dt")
    guv = _matmul(h, w["w_uv"], _ep_gelu, lhs_resident=True, name="in_uv")
    sg = _matmul(h, w["w_g"], _ep_sigmoid, name="in_gates")

    if not sample:
        b, ln, _ = x3.shape
        xbc_act = _conv_prompt(xbc, w["conv_w"], w["conv_b"])
        conv_new = xbc[:, ln - (w["conv_w"].shape[0] - 1):, :]
        dt_g = jnp.pad(dt.reshape(b, ln, groups, hpg), ((0, 0), (0, 0), (0, 0), (0, LANES - hpg)))
        dt_g = dt_g.reshape(b, ln, groups * LANES)
        yn, h_new = _ssd_prompt(xbc_act, dt_g, z, w["alog_g"], w["dsk_e"], w["ssd_norm_g"],
                                d_inner=d_inner, groups=groups, d_state=d_state, headdim=headdim)
        ssm_new = h_new.reshape(b, heads, headdim, d_state)
        yb = _gmlp_prompt(guv, w["gmlp_ln_g"], w["gmlp_ln_b"], w["gmlp_ws"], w["gmlp_bs_t"])
        v_rows = None
    else:
        b = conv0.shape[0]
        ln = x3.shape[1] // b
        cw = w["conv_w"].shape[0]

        def slabs(t):
            return t.reshape(ln, b, t.shape[-1])

        def flat(t):
            return t.reshape(1, ln * b, t.shape[-1])

        xp = jnp.concatenate([jnp.transpose(conv0, (1, 0, 2)), slabs(xbc)], axis=0)
        xbc_act = _conv_slab(xp, w["conv_w"], w["conv_b"])
        conv_new = jnp.transpose(xp[ln:ln + cw - 1], (1, 0, 2))

        def to_rows(t):
            return jnp.pad(jnp.transpose(t, (1, 0, 2)), ((0, 0), (0, SUBLANES - ln), (0, 0)))

        act_b = to_rows(xbc_act)
        dte_b = jnp.repeat(to_rows(slabs(dt)), headdim, axis=2)
        yn_b, h_new = _ssd_sample(act_b[:, :, :d_inner], dte_b, act_b[:, :, d_inner:d_inner + groups * d_state],
                                  act_b[:, :, d_inner + groups * d_state:], to_rows(slabs(z)), w["a_e"], w["dsk_e"],
                                  w["ssd_norm_g"], ssm0.reshape(b, d_inner, d_state), groups=groups, valid=ln)
        ssm_new = h_new.reshape(b, heads, headdim, d_state)
        yn = flat(jnp.transpose(yn_b[:, :ln], (1, 0, 2)).astype(BF16))
        wrow, brow = _slab_gate_rows(w, ln)
        yb, v_tm = _gmlp_slab(slabs(guv), w["gmlp_ln_g"], w["gmlp_ln_b"], wrow, brow)
        yb = flat(yb)
        v_rows = jnp.transpose(v_tm, (1, 0, 2))

    y_a = _matmul(yn, w["w_oa"], _ep_none, name="out_a")
    merged = _matmul(yb, w["w_ob"], _ep_merge, extras=[(y_a, "tile"), (sg, "tile", 0), (sg, "tile", 1)],
                     out_dtype=BF16, name="out_b_merge")
    x1 = _matmul(merged, w["w_out"], _ep_residual, extras=[(x3, "tile"), (mod, "mod", 2)], lhs_resident=True,
                 name="out_proj")
    h2 = _norm_mod(x1, w["norm2_g"], mod, 3, 4)
    return x1, h2, ssm_new, conv_new, v_rows


def _slab_gate_rows(w, ln):
    ws, bs = w["gmlp_ws"], w["gmlp_bs"]
    gw = w["gmlp_ln_g"].shape[1] // ws.shape[0]
    wt = jnp.where(jnp.tril(jnp.ones((ln, ln), bool)), ws[:, :ln, :ln], 0.0)
    wrow = jnp.repeat(jnp.transpose(wt, (1, 2, 0)).reshape(ln * ln, -1), gw, axis=1)
    brow = jnp.repeat(jnp.transpose(bs[:, :ln]), gw, axis=1)
    return wrow, brow


def kernel(x_prompt, x_sample, state_ssm, state_conv, c_prompt, c_sample, w_ada, b_ada, norm1_g, w_in, conv_w, conv_b, dt_bias, a_log, d_skip, ssd_norm_g, w_oa, gmlp_ln_g, gmlp_ln_b, gmlp_ws, gmlp_bs, w_ob, w_out, norm2_g, peer_wq, peer_keys, peer_u, peer_v, final_g):
    depth = w_ada.shape[0]
    bp, lp, d = x_prompt.shape
    bs_, ls, _ = x_sample.shape
    heads = a_log.shape[1]
    headdim, d_state = state_ssm.shape[3], state_ssm.shape[4]
    d_inner = heads * headdim
    conv_dim = conv_w.shape[2]
    groups = (conv_dim - d_inner) // (2 * d_state)
    hpg = heads // groups
    d_gmlp = gmlp_ln_g.shape[1]
    dims = (d, d_inner, groups, d_state, headdim, conv_dim)
    off_xbc = d_inner
    off_dt = off_xbc + conv_dim
    off_u = off_dt + heads
    off_ga = off_u + 2 * d_gmlp

    xp = x_prompt
    xs = jnp.transpose(x_sample, (1, 0, 2)).reshape(1, ls * bs_, d)
    n_c = bs_ + bp
    c_all = jnp.pad(jnp.concatenate([c_sample, c_prompt], axis=0), ((0, (-n_c) % SUBLANES), (0, 0)))
    ssm_p, conv_p, ssm_s, conv_s, v_s = [], [], [], [], []
    for l in range(depth):
        mod = _ada(c_all, w_ada[l], b_ada[l][None, :])
        mod_s = jnp.tile(mod[:bs_], (ls, 1))[None]
        mod_p = mod[bs_:n_c][:, None, :]
        wl = w_in[l]
        a = -jnp.exp(a_log[l])
        w = dict(
            norm1_g=norm1_g[l][None, :], norm2_g=norm2_g[l][None, :],
            w_z=wl[:, :off_xbc].astype(BF16), w_xbc=wl[:, off_xbc:off_dt].astype(BF16),
            w_dt=wl[:, off_dt:off_u].astype(BF16), w_uv=wl[:, off_u:off_ga].astype(BF16),
            w_g=wl[:, off_ga:].astype(BF16), dt_bias=dt_bias[l][None, :],
            conv_w=conv_w[l], conv_b=conv_b[l][None, :],
            alog_g=jnp.pad(a_log[l].reshape(groups, hpg), ((0, 0), (0, LANES - hpg))).reshape(1, groups * LANES),
            a_e=jnp.repeat(a, headdim)[None, :], dsk_e=jnp.repeat(d_skip[l], headdim)[None, :],
            ssd_norm_g=ssd_norm_g[l][None, :],
            w_oa=w_oa[l].astype(BF16), w_ob=w_ob[l].astype(BF16), w_out=w_out[l].astype(BF16),
            gmlp_ln_g=gmlp_ln_g[l][None, :], gmlp_ln_b=gmlp_ln_b[l][None, :],
            gmlp_ws=gmlp_ws[l], gmlp_bs=gmlp_bs[l], gmlp_bs_t=jnp.transpose(gmlp_bs[l]),
        )
        x1p, h2p, hp, cp, _ = _mixer(xp, mod_p, w, dims, sample=False)
        x1s, h2s, hs, cs, vs = _mixer(xs, mod_s, w, dims, sample=True, ssm0=state_ssm[l], conv0=state_conv[l])

        h2 = jnp.concatenate([h2p.reshape(bp * lp, d), h2s.reshape(ls * bs_, d)], axis=0)
        m = h2.shape[0]
        tt = _row_tile(m, 512)
        ne = peer_u.shape[1]
        et = _row_tile(ne, 256)
        scores = _peer_scores(h2, peer_wq[l].astype(BF16), peer_keys[l].astype(BF16), tt)
        stats = _peer_stats(scores, tt)
        peer_t = _peer_mix(jnp.transpose(h2), peer_u[l].astype(BF16), jnp.transpose(peer_v[l]).astype(BF16),
                           scores, stats, tt, et)
        last = l == depth - 1
        fg = final_g[None, :] if last else jnp.ones((1, d), F32)
        assert last, "only DEPTH == 1 is supported"
        xp = _final(x1p, peer_t, 0, mod_p, 5, fg)
        xs = _final(x1s, peer_t, bp * lp, mod_s, 5, fg)
        ssm_p.append(hp)
        conv_p.append(cp)
        ssm_s.append(hs)
        conv_s.append(cs)
        v_s.append(vs)
    y_prompt = xp
    y_sample = jnp.transpose(xs.reshape(ls, bs_, d), (1, 0, 2))
    return (y_prompt, y_sample, jnp.stack(ssm_p), jnp.stack(conv_p), jnp.stack(ssm_s), jnp.stack(conv_s),
            jnp.stack(v_s))
```

```python
import functools

import jax
import jax.numpy as jnp
from jax import lax
from jax.experimental import pallas as pl
from jax.experimental.pallas import tpu as pltpu

F32 = jnp.float32
BF16 = jnp.bfloat16
EPS = 1e-6
N_MOD = 6
PEER_TOPK = 16
SSD_CHUNK = 128
GMLP_CHUNK = 128
SUBLANES = 8
LANES = 128
VMEM_LIMIT = 56 * 1024 * 1024
MATMUL_VMEM_BUDGET = 40 * 1024 * 1024
_SQRT_HALF = 0.7071067811865476
_HIGHEST = lax.Precision.HIGHEST
_NT = (((1,), (1,)), ((), ()))
_TN = (((0,), (0,)), ((), ()))


def _cparams(*sem):
    return pltpu.CompilerParams(dimension_semantics=sem, vmem_limit_bytes=VMEM_LIMIT)


def _silu(x):
    return x * jax.nn.sigmoid(x)


def _gelu(x):
    return 0.5 * x * (1.0 + lax.erf(x * _SQRT_HALF))


def _softplus(x):
    return jnp.maximum(x, 0.0) + jnp.log1p(jnp.exp(-jnp.abs(x)))


def _mean_sq_norm(x):
    return x * lax.rsqrt(jnp.mean(x * x, axis=-1, keepdims=True) + EPS)


def _ada_body(c_ref, w_ref, b_ref, o_ref):
    a = _silu(c_ref[...]).astype(BF16)
    o_ref[...] = jnp.dot(a, w_ref[...].astype(BF16), preferred_element_type=F32) + b_ref[...]


def _ada(c_all, w_ada, b_ada):
    m, d = c_all.shape
    n = w_ada.shape[1]
    tn = min(512, n)
    return pl.pallas_call(
        _ada_body,
        grid=(n // tn,),
        in_specs=[pl.BlockSpec((m, d), lambda j: (0, 0)),
                  pl.BlockSpec((d, tn), lambda j: (0, j)),
                  pl.BlockSpec((1, tn), lambda j: (0, j))],
        out_specs=pl.BlockSpec((m, tn), lambda j: (0, j)),
        out_shape=jax.ShapeDtypeStruct((m, n), F32),
        compiler_params=_cparams("arbitrary"),
        name="ada",
    )(c_all, w_ada, b_ada)


def _mod_block(mod, k, tm, tn):
    nblk = mod.shape[2] // N_MOD // tn
    if mod.shape[1] == 1:
        return (1, 1, tn), lambda j, g, i: (g, 0, k * nblk + j)
    return (1, tm, tn), lambda j, g, i: (0, i, k * nblk + j)


def _mod_spec(mod, k, tm, d):
    shape, index = _mod_block(mod, k, tm, d)
    return pl.BlockSpec(shape, lambda g, i: index(0, g, i))


def _row_tile(r, target):
    t = min(r, target)
    while r % t:
        t //= 2
    return t


def _norm_mod_body(x_ref, g_ref, sc_ref, sh_ref, o_ref):
    y = _mean_sq_norm(x_ref[0]) * g_ref[...]
    o_ref[0] = (y * (1.0 + sc_ref[0]) + sh_ref[0]).astype(o_ref.dtype)


def _norm_mod(x3, g, mod, k_shift, k_scale):
    gg, r, d = x3.shape
    tm = _row_tile(r, 512)
    return pl.pallas_call(
        _norm_mod_body,
        grid=(gg, r // tm),
        in_specs=[pl.BlockSpec((1, tm, d), lambda b, i: (b, i, 0)),
                  pl.BlockSpec((1, d), lambda b, i: (0, 0)),
                  _mod_spec(mod, k_scale, tm, d),
                  _mod_spec(mod, k_shift, tm, d)],
        out_specs=pl.BlockSpec((1, tm, d), lambda b, i: (b, i, 0)),
        out_shape=jax.ShapeDtypeStruct((gg, r, d), BF16),
        compiler_params=_cparams("arbitrary", "arbitrary"),
        name="norm_mod",
    )(x3, g, mod, mod)


def _mm_body(*refs, epilogue, kinds):
    a_ref, w_ref = refs[0], refs[1]
    o_ref = refs[-1]
    acc = jnp.dot(a_ref[0], w_ref[...], preferred_element_type=F32)
    ex = [r[...] if kd == "bias" else r[0] for r, kd in zip(refs[2:-1], kinds)]
    o_ref[0] = epilogue(acc, *ex).astype(o_ref.dtype)


def _matmul(a3, w, epilogue, extras=(), out_dtype=F32, name="mm"):
    gg, r, k = a3.shape
    n = w.shape[1]
    tm = _row_tile(r, 1024)
    tn = _row_tile(n, 1024)
    tile_bytes = sum(e[0].dtype.itemsize for e in extras if e[1] == "tile") + jnp.dtype(out_dtype).itemsize

    def vmem_bytes(tm_, tn_):
        return 2 * (tm_ * k * 2 + k * tn_ * 2 + tm_ * tn_ * tile_bytes) + 2 * tm_ * tn_ * 4

    while vmem_bytes(tm, tn) > MATMUL_VMEM_BUDGET and (tm > 256 or tn > 256):
        if (tm >= tn and tm > 256) or tn <= 256:
            tm //= 2
        else:
            tn //= 2
    in_specs = [pl.BlockSpec((1, tm, k), lambda j, g, i: (g, i, 0)),
                pl.BlockSpec((k, tn), lambda j, g, i: (0, j))]
    args = [a3, w]
    kinds = []
    for e in extras:
        arr, kind = e[0], e[1]
        kinds.append(kind)
        args.append(arr)
        if kind == "tile":
            off = e[2] if len(e) > 2 else 0
            nb = n // tn
            in_specs.append(pl.BlockSpec((1, tm, tn), lambda j, g, i, off=off, nb=nb: (g, i, off * nb + j)))
        elif kind == "bias":
            in_specs.append(pl.BlockSpec((1, tn), lambda j, g, i: (0, j)))
        else:
            in_specs.append(pl.BlockSpec(*_mod_block(arr, e[2], tm, tn)))
    return pl.pallas_call(
        functools.partial(_mm_body, epilogue=epilogue, kinds=tuple(kinds)),
        grid=(n // tn, gg, r // tm),
        in_specs=in_specs,
        out_specs=pl.BlockSpec((1, tm, tn), lambda j, g, i: (g, i, j)),
        out_shape=jax.ShapeDtypeStruct((gg, r, n), out_dtype),
        compiler_params=_cparams("arbitrary", "arbitrary", "arbitrary"),
        name=name,
    )(*args)


def _ep_none(acc):
    return acc


def _ep_gelu(acc):
    return _gelu(acc)


def _ep_sigmoid(acc):
    return jax.nn.sigmoid(acc)


def _ep_softplus_bias(acc, bias):
    return _softplus(acc + bias)


def _ep_merge(acc, y_a, sg_a, sg_b):
    return sg_a * y_a + sg_b * acc


def _ep_residual(acc, x, gate):
    return x + gate * acc


def _conv_prompt_body(x_ref, w_ref, b_ref, o_ref, pad_ref, *, width):
    ln = x_ref.shape[1]
    pad_ref[0:SUBLANES, :] = jnp.zeros((SUBLANES, pad_ref.shape[1]), F32)
    pad_ref[SUBLANES:SUBLANES + ln, :] = x_ref[0]
    acc = b_ref[...] + x_ref[0] * w_ref[width - 1:width, :]
    for k in range(width - 1):
        s = SUBLANES - (width - 1) + k
        acc = acc + pad_ref[s:s + ln, :] * w_ref[k:k + 1, :]
    o_ref[0] = _silu(acc)


def _conv_prompt(x3, conv_w, conv_b):
    b, ln, c = x3.shape
    width = conv_w.shape[0]
    ct = _row_tile(c, 512)
    return pl.pallas_call(
        functools.partial(_conv_prompt_body, width=width),
        grid=(b, c // ct),
        in_specs=[pl.BlockSpec((1, ln, ct), lambda i, j: (i, 0, j)),
                  pl.BlockSpec((width, ct), lambda i, j: (0, j)),
                  pl.BlockSpec((1, ct), lambda i, j: (0, j))],
        out_specs=pl.BlockSpec((1, ln, ct), lambda i, j: (i, 0, j)),
        out_shape=jax.ShapeDtypeStruct((b, ln, c), F32),
        scratch_shapes=[pltpu.VMEM((ln + SUBLANES, ct), F32)],
        compiler_params=_cparams("arbitrary", "arbitrary"),
        name="conv_prompt",
    )(x3, conv_w, conv_b)


def _conv_slab_body(xp_ref, w_ref, b_ref, o_ref, *, width):
    for t in range(o_ref.shape[0]):
        acc = b_ref[...] + xp_ref[t] * w_ref[0:1, :]
        for k in range(1, width):
            acc = acc + xp_ref[t + k] * w_ref[k:k + 1, :]
        o_ref[t] = _silu(acc)


def _conv_slab(xp, conv_w, conv_b):
    lp, b, c = xp.shape
    width = conv_w.shape[0]
    ln = lp - width + 1
    ct = _row_tile(c, 1024)
    return pl.pallas_call(
        functools.partial(_conv_slab_body, width=width),
        grid=(c // ct,),
        in_specs=[pl.BlockSpec((lp, b, ct), lambda j: (0, 0, j)),
                  pl.BlockSpec((width, ct), lambda j: (0, j)),
                  pl.BlockSpec((1, ct), lambda j: (0, j))],
        out_specs=pl.BlockSpec((ln, b, ct), lambda j: (0, 0, j)),
        out_shape=jax.ShapeDtypeStruct((ln, b, c), F32),
        compiler_params=_cparams("arbitrary"),
        name="conv_sample",
    )(xp, conv_w, conv_b)


def _ssd_prompt_body(x_ref, b_ref, c_ref, dt_ref, z_ref, alog_ref, dsk_ref, ng_ref, y_ref, hl_ref,
                     h_scr, y_scr, *, headdim):
    ci = pl.program_id(2)

    @pl.when(ci == 0)
    def _():
        h_scr[...] = jnp.zeros_like(h_scr)

    x = x_ref[0]
    cl, gw = x.shape
    dt = dt_ref[0]
    a = -jnp.exp(alog_ref[...])
    row = lax.broadcasted_iota(jnp.int32, (cl, cl), 0)
    col = lax.broadcasted_iota(jnp.int32, (cl, cl), 1)
    tril = row >= col
    acum = jnp.dot(tril.astype(F32), dt * a, precision=_HIGHEST, preferred_element_type=F32)
    acum_t = acum.T
    lane_head = lax.broadcasted_iota(jnp.int32, (cl, LANES), 1) // headdim
    heads_per_blk = LANES // headdim
    bb = b_ref[0].astype(BF16)
    cc = c_ref[0].astype(BF16)
    cb = lax.dot_general(cc, bb, _NT, preferred_element_type=F32)
    h_t = h_scr[...]
    y_off = jnp.dot(cc, h_t.astype(BF16), preferred_element_type=F32)
    for j in range(gw // LANES):
        sl = slice(j * LANES, (j + 1) * LANES)
        h0 = j * heads_per_blk
        dte = jnp.broadcast_to(dt[:, h0:h0 + 1], (cl, LANES))
        ace = jnp.broadcast_to(acum[:, h0:h0 + 1], (cl, LANES))
        for q in range(1, heads_per_blk):
            dte = jnp.where(lane_head == q, dt[:, h0 + q:h0 + q + 1], dte)
            ace = jnp.where(lane_head == q, acum[:, h0 + q:h0 + q + 1], ace)
        xj = x[:, sl]
        xdt = xj * dte
        last = ace[cl - 1:cl, :]
        yd = jnp.zeros((cl, LANES), F32)
        for q in range(heads_per_blk):
            hh = h0 + q
            seg = acum[:, hh:hh + 1] - acum_t[hh:hh + 1, :]
            m = (cb * jnp.exp(jnp.where(tril, seg, -jnp.inf))).astype(BF16)
            xm = jnp.where(lane_head == q, xdt, 0.0).astype(BF16)
            yd = yd + jnp.dot(m, xm, preferred_element_type=F32)
        y = yd + y_off[:, sl] * jnp.exp(ace) + dsk_ref[:, sl] * xj
        y_scr[:, sl] = y * _silu(z_ref[0][:, sl])
        st_t = lax.dot_general(bb, (xdt * jnp.exp(last - ace)).astype(BF16), _TN,
                               preferred_element_type=F32)
        h_scr[:, sl] = h_t[:, sl] * jnp.exp(last) + st_t
    y_ref[0] = (_mean_sq_norm(y_scr[...]) * ng_ref[...]).astype(y_ref.dtype)

    @pl.when(ci == pl.num_programs(2) - 1)
    def _():
        hl_ref[0] = h_scr[...].T


def _ssd_prompt(xbc_act, dt_g, z, alog_g, dsk_e, norm_g, *, d_inner, groups, d_state, headdim):
    b, ln, _ = xbc_act.shape
    gw = d_inner // groups
    cl = min(SSD_CHUNK, ln)
    nc = ln // cl
    xb = d_inner // d_state
    return pl.pallas_call(
        functools.partial(_ssd_prompt_body, headdim=headdim),
        grid=(b, groups, nc),
        in_specs=[pl.BlockSpec((1, cl, gw), lambda i, g, c: (i, c, g)),
                  pl.BlockSpec((1, cl, d_state), lambda i, g, c: (i, c, xb + g)),
                  pl.BlockSpec((1, cl, d_state), lambda i, g, c: (i, c, xb + groups + g)),
                  pl.BlockSpec((1, cl, LANES), lambda i, g, c: (i, c, g)),
                  pl.BlockSpec((1, cl, gw), lambda i, g, c: (i, c, g)),
                  pl.BlockSpec((1, LANES), lambda i, g, c: (0, g)),
                  pl.BlockSpec((1, gw), lambda i, g, c: (0, g)),
                  pl.BlockSpec((1, gw), lambda i, g, c: (0, g))],
        out_specs=[pl.BlockSpec((1, cl, gw), lambda i, g, c: (i, c, g)),
                   pl.BlockSpec((1, gw, d_state), lambda i, g, c: (i, g, 0))],
        out_shape=[jax.ShapeDtypeStruct((b, ln, d_inner), BF16),
                   jax.ShapeDtypeStruct((b, d_inner, d_state), F32)],
        scratch_shapes=[pltpu.VMEM((d_state, gw), F32), pltpu.VMEM((cl, gw), F32)],
        compiler_params=_cparams("arbitrary", "arbitrary", "arbitrary"),
        name="ssd_prompt",
    )(xbc_act, xbc_act, xbc_act, dt_g, z, alog_g, dsk_e, norm_g)


def _ssd_sample_body(x_ref, dte_ref, b_ref, c_ref, z_ref, ae_ref, dsk_ref, ng_ref, h0_ref, y_ref, hn_ref,
                     *, groups, valid):
    x = x_ref[0]
    rows, di = x.shape
    gw = di // groups
    n = h0_ref.shape[2]
    rix = lax.broadcasted_iota(jnp.int32, (rows, di), 0)
    dte = jnp.where(rix < valid, dte_ref[0], 0.0)
    ac = dte * ae_ref[...]
    for sh in (1, 2, 4):
        ac = ac + jnp.where(rix >= sh, pltpu.roll(ac, sh, axis=0), 0.0)
    xdt = x * dte
    last = ac[rows - 1:rows, :]
    cdec = jnp.exp(last)
    hi = cdec.astype(BF16)
    r1 = cdec - hi.astype(F32)
    mid = r1.astype(BF16)
    lo = (r1 - mid.astype(F32)).astype(BF16)
    lhs = (xdt * jnp.exp(last - ac)).astype(BF16)
    lhs = jnp.where(rix == valid, hi, jnp.where(rix == valid + 1, mid, jnp.where(rix == valid + 2, lo, lhs)))
    rn = lax.broadcasted_iota(jnp.int32, (rows, n), 0)
    ones_rows = jnp.where((rn >= valid) & (rn < valid + 3), 1.0, 0.0).astype(BF16)
    rg = lax.broadcasted_iota(jnp.int32, (rows, gw), 0)
    eac = jnp.exp(ac)
    for g in range(groups):
        sl = slice(g * gw, (g + 1) * gw)
        bg = b_ref[0][:, g * n:(g + 1) * n]
        cg = c_ref[0][:, g * n:(g + 1) * n]
        h0 = h0_ref[0, sl, :]
        y_off = lax.dot_general(cg.astype(BF16), h0.astype(BF16), _NT, preferred_element_type=F32)
        rhs = jnp.concatenate([jnp.where(rn < valid, bg, 0.0).astype(BF16), ones_rows], axis=1)
        res = lax.dot_general(lhs[:, sl], rhs, _TN, preferred_element_type=F32)
        hn_ref[0, sl, :] = h0 * res[:, n:] + res[:, :n]
        acg = ac[:, sl]
        xdtg = xdt[:, sl]
        yd = jnp.zeros((rows, gw), F32)
        for s in range(valid):
            cbs = jnp.sum(cg * bg[s:s + 1, :], axis=-1, keepdims=True)
            w = jnp.where(rg >= s, jnp.exp(acg - acg[s:s + 1, :]), 0.0)
            yd = yd + cbs * w * xdtg[s:s + 1, :]
        xg = x[:, sl]
        y = (yd + y_off * eac[:, sl] + dsk_ref[:, sl] * xg) * _silu(z_ref[0][:, sl])
        y_ref[0, :, sl] = (_mean_sq_norm(y) * ng_ref[:, sl]).astype(y_ref.dtype)


def _ssd_sample(xbc_b, dte_b, z_b, a_e, dsk_e, norm_g, h0, *, groups, valid):
    b, rows, _ = xbc_b.shape
    _, di, n = h0.shape
    gn = groups * n
    assert valid + 3 <= rows and di % gn == 0
    seq = lambda i: (i, 0, 0)
    par = lambda i: (0, 0)
    return pl.pallas_call(
        functools.partial(_ssd_sample_body, groups=groups, valid=valid),
        grid=(b,),
        in_specs=[pl.BlockSpec((1, rows, di), seq),
                  pl.BlockSpec((1, rows, di), seq),
                  pl.BlockSpec((1, rows, gn), lambda i: (i, 0, di // gn)),
                  pl.BlockSpec((1, rows, gn), lambda i: (i, 0, di // gn + 1)),
                  pl.BlockSpec((1, rows, di), seq),
                  pl.BlockSpec((1, di), par),
                  pl.BlockSpec((1, di), par),
                  pl.BlockSpec((1, di), par),
                  pl.BlockSpec((1, di, n), seq)],
        out_specs=[pl.BlockSpec((1, rows, di), seq),
                   pl.BlockSpec((1, di, n), seq)],
        out_shape=[jax.ShapeDtypeStruct((b, rows, di), F32),
                   jax.ShapeDtypeStruct((b, di, n), F32)],
        compiler_params=_cparams("arbitrary"),
        name="ssd_sample",
    )(xbc_b, dte_b, xbc_b, xbc_b, z_b, a_e, dsk_e, norm_g, h0)


def _layernorm(v, g, b):
    mu = jnp.mean(v, axis=-1, keepdims=True)
    d = v - mu
    var = jnp.mean(d * d, axis=-1, keepdims=True)
    return d * lax.rsqrt(var + EPS) * g + b


def _gmlp_prompt_body(gv_ref, gu_ref, lg_ref, lb_ref, ws_ref, bst_ref, o_ref):
    vn = _layernorm(gv_ref[0], lg_ref[...], lb_ref[...])
    cl, d = vn.shape
    groups = ws_ref.shape[0]
    gw = d // groups
    row = lax.broadcasted_iota(jnp.int32, (cl, cl), 0)
    col = lax.broadcasted_iota(jnp.int32, (cl, cl), 1)
    tril = row >= col
    for g in range(groups):
        sl = slice(g * gw, (g + 1) * gw)
        w = jnp.where(tril, ws_ref[g], 0.0).astype(BF16)
        s = jnp.dot(w, vn[:, sl].astype(BF16), preferred_element_type=F32) + bst_ref[:, g:g + 1]
        o_ref[0, :, sl] = (gu_ref[0][:, sl] * s).astype(o_ref.dtype)


def _gmlp_prompt(guv, ln_g, ln_b, ws, bs_t):
    b, ln, d2 = guv.shape
    d = d2 // 2
    cl = min(GMLP_CHUNK, ln)
    groups = ws.shape[0]
    return pl.pallas_call(
        _gmlp_prompt_body,
        grid=(b, ln // cl),
        in_specs=[pl.BlockSpec((1, cl, d), lambda i, c: (i, c, 1)),
                  pl.BlockSpec((1, cl, d), lambda i, c: (i, c, 0)),
                  pl.BlockSpec((1, d), lambda i, c: (0, 0)),
                  pl.BlockSpec((1, d), lambda i, c: (0, 0)),
                  pl.BlockSpec((groups, cl, cl), lambda i, c: (0, 0, 0)),
                  pl.BlockSpec((cl, groups), lambda i, c: (0, 0))],
        out_specs=pl.BlockSpec((1, cl, d), lambda i, c: (i, c, 0)),
        out_shape=jax.ShapeDtypeStruct((b, ln, d), BF16),
        compiler_params=_cparams("arbitrary", "arbitrary"),
        name="gmlp_prompt",
    )(guv, guv, ln_g, ln_b, ws, bs_t)


def _gmlp_slab_body(gv_ref, gu_ref, lg_ref, lb_ref, wrow_ref, brow_ref, o_ref, vn_ref):
    ln = gv_ref.shape[0]
    vn = [_layernorm(gv_ref[t], lg_ref[...], lb_ref[...]) for t in range(ln)]
    for t in range(ln):
        vn_ref[t] = vn[t]
        s = brow_ref[t:t + 1, :] + vn[0] * wrow_ref[t * ln:t * ln + 1, :]
        for k in range(1, t + 1):
            s = s + vn[k] * wrow_ref[t * ln + k:t * ln + k + 1, :]
        o_ref[t] = (gu_ref[t] * s).astype(o_ref.dtype)


def _gmlp_slab(guv, ln_g, ln_b, wrow, brow):
    ln, b, d2 = guv.shape
    d = d2 // 2
    bt = _row_tile(b, 32)
    return pl.pallas_call(
        _gmlp_slab_body,
        grid=(b // bt,),
        in_specs=[pl.BlockSpec((ln, bt, d), lambda i: (0, i, 1)),
                  pl.BlockSpec((ln, bt, d), lambda i: (0, i, 0)),
                  pl.BlockSpec((1, d), lambda i: (0, 0)),
                  pl.BlockSpec((1, d), lambda i: (0, 0)),
                  pl.BlockSpec((ln * ln, d), lambda i: (0, 0)),
                  pl.BlockSpec((ln, d), lambda i: (0, 0))],
        out_specs=[pl.BlockSpec((ln, bt, d), lambda i: (0, i, 0)),
                   pl.BlockSpec((ln, bt, d), lambda i: (0, i, 0))],
        out_shape=[jax.ShapeDtypeStruct((ln, b, d), BF16),
                   jax.ShapeDtypeStruct((ln, b, d), F32)],
        compiler_params=_cparams("arbitrary"),
        name="gmlp_sample",
    )(guv, guv, ln_g, ln_b, wrow, brow)


def _peer_scores_body(h_ref, wq_ref, k_ref, o_ref):
    q = jnp.dot(h_ref[...], wq_ref[...], preferred_element_type=F32).astype(BF16)
    hb, _, _, half = k_ref.shape
    for hh in range(hb):
        for i in range(2):
            c0 = (hh * 2 + i) * half
            o_ref[hh, i] = lax.dot_general(k_ref[hh, i], q[:, c0:c0 + half], _NT, preferred_element_type=F32)


def _peer_scores(h2, wq, keys, tt):
    m, d = h2.shape
    heads, _, nk, half = keys.shape
    hb = _row_tile(heads, 4)
    return pl.pallas_call(
        _peer_scores_body,
        grid=(heads // hb, m // tt),
        in_specs=[pl.BlockSpec((tt, d), lambda h, t: (t, 0)),
                  pl.BlockSpec((d, hb * 2 * half), lambda h, t: (0, h)),
                  pl.BlockSpec((hb, 2, nk, half), lambda h, t: (h, 0, 0, 0))],
        out_specs=pl.BlockSpec((hb, 2, nk, tt), lambda h, t: (h, 0, 0, t)),
        out_shape=jax.ShapeDtypeStruct((heads, 2, nk, m), F32),
        compiler_params=_cparams("arbitrary", "arbitrary"),
        name="peer_scores",
    )(h2, wq, keys)


def _top_values(s, k):
    n = s.shape[0]
    ridx = lax.broadcasted_iota(jnp.int32, s.shape, 0)
    kidx = lax.broadcasted_iota(jnp.int32, (k, s.shape[1]), 0)
    out = jnp.zeros((k, s.shape[1]), F32)
    for r in range(k):
        m = jnp.max(s, axis=0, keepdims=True)
        first = jnp.min(jnp.where(s == m, ridx, n), axis=0, keepdims=True)
        s = jnp.where(ridx == first, -jnp.inf, s)
        out = jnp.where(kidx == r, m, out)
    return out


def _sorted_top_values(s, k):
    n, t = s.shape
    assert n == SUBLANES * k and k & (k - 1) == 0
    x = [s[r * SUBLANES:(r + 1) * SUBLANES, :] for r in range(k)]

    def merge_descending(c):
        stride = k // 2
        while stride >= 1:
            for i in range(k):
                l = i ^ stride
                if l > i:
                    c[i], c[l] = jnp.maximum(c[i], c[l]), jnp.minimum(c[i], c[l])
            stride //= 2
        return c

    size = 2
    while size <= k:
        stride = size // 2
        while stride >= 1:
            for i in range(k):
                l = i ^ stride
                if l > i:
                    hi, lo = jnp.maximum(x[i], x[l]), jnp.minimum(x[i], x[l])
                    x[i], x[l] = (hi, lo) if (i & size) == 0 else (lo, hi)
            stride //= 2
        size *= 2
    shift = SUBLANES // 2
    while shift >= 1:
        y = [pltpu.roll(a, shift, axis=0) for a in x]
        x = merge_descending([jnp.maximum(x[i], y[k - 1 - i]) for i in range(k)])
        shift //= 2
    row = lax.broadcasted_iota(jnp.int32, (SUBLANES, t), 0)
    halves = []
    for h0 in range(0, k, SUBLANES):
        acc = x[h0]
        for r in range(1, SUBLANES):
            acc = jnp.where(row == r, x[h0 + r], acc)
        halves.append(acc)
    return jnp.concatenate(halves, axis=0) if len(halves) > 1 else halves[0]


def _pair_candidates(v1, v2, k):
    isplit = 4
    lanes = v1.shape[1]

    def rows(nr):
        return lax.broadcasted_iota(jnp.int32, (nr, lanes), 0)

    pieces = []
    for i in range(min(isplit, k)):
        nj = k // (i + 1)
        nr = -(-nj // SUBLANES) * SUBLANES
        p = v1[i:i + 1, :] + v2[0:nr, :]
        pieces.append(p if nr == nj else jnp.where(rows(nr) < nj, p, -jnp.inf))
    for j in range(k // (isplit + 1)):
        ni = k // (j + 1)
        nr = -(-ni // SUBLANES) * SUBLANES
        r = rows(nr)
        pieces.append(jnp.where((r >= isplit) & (r < ni), v1[0:nr, :] + v2[j:j + 1, :], -jnp.inf))
    return jnp.concatenate(pieces, axis=0)


def _peer_stats_body(s_ref, o_ref, *, topk):
    heads, _, nk, _ = s_ref.shape
    sub_key_top = _sorted_top_values if nk == SUBLANES * topk else _top_values
    for h in range(heads):
        v1 = sub_key_top(s_ref[h, 0], topk)
        v2 = sub_key_top(s_ref[h, 1], topk)
        top = _top_values(_pair_candidates(v1, v2, topk), topk)
        mx = top[0:1, :]
        lse = mx + jnp.log(jnp.sum(jnp.exp(top - mx), axis=0, keepdims=True))
        o_ref[h, 0:1, :] = top[topk - 1:topk, :]
        o_ref[h, 1:2, :] = lse


def _peer_stats(scores, tt):
    heads, _, nk, m = scores.shape
    return pl.pallas_call(
        functools.partial(_peer_stats_body, topk=PEER_TOPK),
        grid=(m // tt,),
        in_specs=[pl.BlockSpec((heads, 2, nk, tt), lambda t: (0, 0, 0, t))],
        out_specs=pl.BlockSpec((heads, 2, tt), lambda t: (0, 0, t)),
        out_shape=jax.ShapeDtypeStruct((heads, 2, m), F32),
        compiler_params=_cparams("arbitrary"),
        name="peer_stats",
    )(scores)


def _peer_mix_body(h_ref, u_ref, v_ref, s_ref, st_ref, o_ref, *, nk):
    e = pl.program_id(1)

    @pl.when(e == 0)
    def _():
        o_ref[...] = jnp.zeros_like(o_ref)

    et = u_ref.shape[0]
    heads = s_ref.shape[0]
    act = _gelu(lax.dot_general(u_ref[...], h_ref[...], _NT, preferred_element_type=F32))
    parts = []
    for ii in range(et // nk):
        i = e * (et // nk) + ii
        wsum = jnp.zeros((nk, act.shape[1]), F32)
        for h in range(heads):
            c = s_ref[h, 0, pl.ds(i, 1), :] + s_ref[h, 1]
            wsum = wsum + jnp.where(c >= st_ref[h, 0:1, :], jnp.exp(c - st_ref[h, 1:2, :]), 0.0)
        parts.append((wsum * act[ii * nk:(ii + 1) * nk, :]).astype(BF16))
    wg = jnp.concatenate(parts, axis=0) if len(parts) > 1 else parts[0]
    o_ref[...] += lax.dot_general(wg, v_ref[...].astype(BF16), _TN, preferred_element_type=F32)


def _peer_mix(h2, u, v, scores, stats, tt, et):
    m, d = h2.shape
    ne = u.shape[0]
    heads, _, nk, _ = scores.shape
    return pl.pallas_call(
        functools.partial(_peer_mix_body, nk=nk),
        grid=(m // tt, ne // et),
        in_specs=[pl.BlockSpec((tt, d), lambda t, e: (t, 0)),
                  pl.BlockSpec((et, d), lambda t, e: (e, 0)),
                  pl.BlockSpec((et, d), lambda t, e: (e, 0)),
                  pl.BlockSpec((heads, 2, nk, tt), lambda t, e: (0, 0, 0, t)),
                  pl.BlockSpec((heads, 2, tt), lambda t, e: (0, 0, t))],
        out_specs=pl.BlockSpec((tt, d), lambda t, e: (t, 0)),
        out_shape=jax.ShapeDtypeStruct((m, d), F32),
        compiler_params=_cparams("arbitrary", "arbitrary"),
        name="peer_mix",
    )(h2, u, v, scores, stats)


def _final_body(x_ref, p_ref, gate_ref, fg_ref, o_ref):
    xn = x_ref[0] + gate_ref[0] * p_ref[...]
    o_ref[0] = _mean_sq_norm(xn) * fg_ref[...]


def _final(x3, peer, row0, mod, k_gate, final_g):
    gg, r, d = x3.shape
    tm = _row_tile(r, 256)
    nb = r // tm
    r0 = row0 // tm
    return pl.pallas_call(
        _final_body,
        grid=(gg, nb),
        in_specs=[pl.BlockSpec((1, tm, d), lambda b, i: (b, i, 0)),
                  pl.BlockSpec((tm, d), lambda b, i: (r0 + b * nb + i, 0)),
                  _mod_spec(mod, k_gate, tm, d),
                  pl.BlockSpec((1, d), lambda b, i: (0, 0))],
        out_specs=pl.BlockSpec((1, tm, d), lambda b, i: (b, i, 0)),
        out_shape=jax.ShapeDtypeStruct((gg, r, d), F32),
        compiler_params=_cparams("arbitrary", "arbitrary"),
        name="final",
    )(x3, peer, mod, final_g)


def _mixer(x3, mod, w, dims, *, sample, ssm0=None, conv0=None):
    d, d_inner, groups, d_state, headdim, conv_dim = dims
    heads = d_inner // headdim
    hpg = heads // groups
    h = _norm_mod(x3, w["norm1_g"], mod, 0, 1)
    z = _matmul(h, w["w_z"], _ep_none, name="in_z")
    xbc = _matmul(h, w["w_xbc"], _ep_none, name="in_xbc")
    dt = _matmul(h, w["w_dt"], _ep_softplus_bias, extras=[(w["dt_bias"], "bias")], name="in_dt")
    guv = _matmul(h, w["w_uv"], _ep_gelu, name="in_uv")
    sg = _matmul(h, w["w_g"], _ep_sigmoid, name="in_gates")

    if not sample:
        b, ln, _ = x3.shape
        xbc_act = _conv_prompt(xbc, w["conv_w"], w["conv_b"])
        conv_new = xbc[:, ln - (w["conv_w"].shape[0] - 1):, :]
        dt_g = jnp.pad(dt.reshape(b, ln, groups, hpg), ((0, 0), (0, 0), (0, 0), (0, LANES - hpg)))
        dt_g = dt_g.reshape(b, ln, groups * LANES)
        yn, h_new = _ssd_prompt(xbc_act, dt_g, z, w["alog_g"], w["dsk_e"], w["ssd_norm_g"],
                                d_inner=d_inner, groups=groups, d_state=d_state, headdim=headdim)
        ssm_new = h_new.reshape(b, heads, headdim, d_state)
        yb = _gmlp_prompt(guv, w["gmlp_ln_g"], w["gmlp_ln_b"], w["gmlp_ws"], w["gmlp_bs_t"])
        v_rows = None
    else:
        b = conv0.shape[0]
        ln = x3.shape[1] // b
        cw = w["conv_w"].shape[0]

        def slabs(t):
            return t.reshape(ln, b, t.shape[-1])

        def flat(t):
            return t.reshape(1, ln * b, t.shape[-1])

        xp = jnp.concatenate([jnp.transpose(conv0, (1, 0, 2)), slabs(xbc)], axis=0)
        xbc_act = _conv_slab(xp, w["conv_w"], w["conv_b"])
        conv_new = jnp.transpose(xp[ln:ln + cw - 1], (1, 0, 2))

        def to_rows(t):
            return jnp.pad(jnp.transpose(t, (1, 0, 2)), ((0, 0), (0, SUBLANES - ln), (0, 0)))

        dte_b = jnp.repeat(to_rows(slabs(dt)), headdim, axis=2)
        yn_b, h_new = _ssd_sample(to_rows(xbc_act), dte_b, to_rows(slabs(z)), w["a_e"], w["dsk_e"],
                                  w["ssd_norm_g"], ssm0.reshape(b, d_inner, d_state), groups=groups, valid=ln)
        ssm_new = h_new.reshape(b, heads, headdim, d_state)
        yn = flat(jnp.transpose(yn_b[:, :ln], (1, 0, 2)).astype(BF16))
        wrow, brow = _slab_gate_rows(w, ln)
        yb, v_tm = _gmlp_slab(slabs(guv), w["gmlp_ln_g"], w["gmlp_ln_b"], wrow, brow)
        yb = flat(yb)
        v_rows = jnp.transpose(v_tm, (1, 0, 2))

    y_a = _matmul(yn, w["w_oa"], _ep_none, name="out_a")
    merged = _matmul(yb, w["w_ob"], _ep_merge, extras=[(y_a, "tile"), (sg, "tile", 0), (sg, "tile", 1)],
                     out_dtype=BF16, name="out_b_merge")
    x1 = _matmul(merged, w["w_out"], _ep_residual, extras=[(x3, "tile"), (mod, "mod", 2)], name="out_proj")
    h2 = _norm_mod(x1, w["norm2_g"], mod, 3, 4)
    return x1, h2, ssm_new, conv_new, v_rows


def _slab_gate_rows(w, ln):
    ws, bs = w["gmlp_ws"], w["gmlp_bs"]
    gw = w["gmlp_ln_g"].shape[1] // ws.shape[0]
    wt = jnp.where(jnp.tril(jnp.ones((ln, ln), bool)), ws[:, :ln, :ln], 0.0)
    wrow = jnp.repeat(jnp.transpose(wt, (1, 2, 0)).reshape(ln * ln, -1), gw, axis=1)
    brow = jnp.repeat(jnp.transpose(bs[:, :ln]), gw, axis=1)
    return wrow, brow


def kernel(x_prompt, x_sample, state_ssm, state_conv, c_prompt, c_sample, w_ada, b_ada, norm1_g, w_in, conv_w, conv_b, dt_bias, a_log, d_skip, ssd_norm_g, w_oa, gmlp_ln_g, gmlp_ln_b, gmlp_ws, gmlp_bs, w_ob, w_out, norm2_g, peer_wq, peer_keys, peer_u, peer_v, final_g):
    depth = w_ada.shape[0]
    bp, lp, d = x_prompt.shape
    bs_, ls, _ = x_sample.shape
    heads = a_log.shape[1]
    headdim, d_state = state_ssm.shape[3], state_ssm.shape[4]
    d_inner = heads * headdim
    conv_dim = conv_w.shape[2]
    groups = (conv_dim - d_inner) // (2 * d_state)
    hpg = heads // groups
    d_gmlp = gmlp_ln_g.shape[1]
    dims = (d, d_inner, groups, d_state, headdim, conv_dim)
    off_xbc = d_inner
    off_dt = off_xbc + conv_dim
    off_u = off_dt + heads
    off_ga = off_u + 2 * d_gmlp

    xp = x_prompt
    xs = jnp.transpose(x_sample, (1, 0, 2)).reshape(1, ls * bs_, d)
    n_c = bs_ + bp
    c_all = jnp.pad(jnp.concatenate([c_sample, c_prompt], axis=0), ((0, (-n_c) % SUBLANES), (0, 0)))
    ssm_p, conv_p, ssm_s, conv_s, v_s = [], [], [], [], []
    for l in range(depth):
        mod = _ada(c_all, w_ada[l], b_ada[l][None, :])
        mod_s = jnp.tile(mod[:bs_], (ls, 1))[None]
        mod_p = mod[bs_:n_c][:, None, :]
        wl = w_in[l]
        a = -jnp.exp(a_log[l])
        w = dict(
            norm1_g=norm1_g[l][None, :], norm2_g=norm2_g[l][None, :],
            w_z=wl[:, :off_xbc].astype(BF16), w_xbc=wl[:, off_xbc:off_dt].astype(BF16),
            w_dt=wl[:, off_dt:off_u].astype(BF16), w_uv=wl[:, off_u:off_ga].astype(BF16),
            w_g=wl[:, off_ga:].astype(BF16), dt_bias=dt_bias[l][None, :],
            conv_w=conv_w[l], conv_b=conv_b[l][None, :],
            alog_g=jnp.pad(a_log[l].reshape(groups, hpg), ((0, 0), (0, LANES - hpg))).reshape(1, groups * LANES),
            a_e=jnp.repeat(a, headdim)[None, :], dsk_e=jnp.repeat(d_skip[l], headdim)[None, :],
            ssd_norm_g=ssd_norm_g[l][None, :],
            w_oa=w_oa[l].astype(BF16), w_ob=w_ob[l].astype(BF16), w_out=w_out[l].astype(BF16),
            gmlp_ln_g=gmlp_ln_g[l][None, :], gmlp_ln_b=gmlp_ln_b[l][None, :],
            gmlp_ws=gmlp_ws[l], gmlp_bs=gmlp_bs[l], gmlp_bs_t=jnp.transpose(gmlp_bs[l]),
        )
        x1p, h2p, hp, cp, _ = _mixer(xp, mod_p, w, dims, sample=False)
        x1s, h2s, hs, cs, vs = _mixer(xs, mod_s, w, dims, sample=True, ssm0=state_ssm[l], conv0=state_conv[l])

        h2 = jnp.concatenate([h2p.reshape(bp * lp, d), h2s.reshape(ls * bs_, d)], axis=0)
        m = h2.shape[0]
        tt = _row_tile(m, 512)
        ne = peer_u.shape[1]
        et = _row_tile(ne, 256)
        scores = _peer_scores(h2, peer_wq[l].astype(BF16), peer_keys[l].astype(BF16), tt)
        stats = _peer_stats(scores, tt)
        peer_t = _peer_mix(h2, peer_u[l].astype(BF16), peer_v[l], scores, stats, tt, et)
        last = l == depth - 1
        fg = final_g[None, :] if last else jnp.ones((1, d), F32)
        assert last, "only DEPTH == 1 is supported"
        xp = _final(x1p, peer_t, 0, mod_p, 5, fg)
        xs = _final(x1s, peer_t, bp * lp, mod_s, 5, fg)
        ssm_p.append(hp)
        conv_p.append(cp)
        ssm_s.append(hs)
        conv_s.append(cs)
        v_s.append(vs)
    y_prompt = xp
    y_sample = jnp.transpose(xs.reshape(ls, bs_, d), (1, 0, 2))
    return (y_prompt, y_sample, jnp.stack(ssm_p), jnp.stack(conv_p), jnp.stack(ssm_s), jnp.stack(conv_s),
            jnp.stack(v_s))
```

```python
import functools
import math

import jax
import jax.numpy as jnp
from jax import lax
from jax.experimental import pallas as pl
from jax.experimental.pallas import tpu as pltpu

F32 = jnp.float32
BF16 = jnp.bfloat16
EPS = 1e-6
N_MOD = 6
PEER_TOPK = 16
PEER_STATS = 2
SSD_CHUNK = 128
GMLP_CHUNK = 128
SUBLANES = 8
LANES = 128
VMEM_LIMIT = 56 * 1024 * 1024
MATMUL_VMEM_BUDGET = 40 * 1024 * 1024
FEATURE_TILE = 1024
_SQRT_HALF = 0.7071067811865476
_HIGHEST = lax.Precision.HIGHEST
_NT = (((1,), (1,)), ((), ()))
_TN = (((0,), (0,)), ((), ()))


def _cparams(*sem):
    return pltpu.CompilerParams(dimension_semantics=sem, vmem_limit_bytes=VMEM_LIMIT)


def _silu(x):
    return x * jax.nn.sigmoid(x)


def _gelu(x):
    return 0.5 * x * (1.0 + lax.erf(x * _SQRT_HALF))


def _softplus(x):
    return jnp.maximum(x, 0.0) + jnp.log1p(jnp.exp(-jnp.abs(x)))


def _mean_sq_norm(x):
    return x * lax.rsqrt(jnp.mean(x * x, axis=-1, keepdims=True) + EPS)


def _ada_body(c_ref, w_ref, b_ref, o_ref):
    a = _silu(c_ref[...]).astype(BF16)
    o_ref[...] = jnp.dot(a, w_ref[...].astype(BF16), preferred_element_type=F32) + b_ref[...]


def _ada(c_all, w_ada, b_ada):
    m, d = c_all.shape
    n = w_ada.shape[1]
    tn = min(512, n)
    return pl.pallas_call(
        _ada_body,
        grid=(n // tn,),
        in_specs=[pl.BlockSpec((m, d), lambda j: (0, 0)),
                  pl.BlockSpec((d, tn), lambda j: (0, j)),
                  pl.BlockSpec((1, tn), lambda j: (0, j))],
        out_specs=pl.BlockSpec((m, tn), lambda j: (0, j)),
        out_shape=jax.ShapeDtypeStruct((m, n), F32),
        compiler_params=_cparams("arbitrary"),
        name="ada",
    )(c_all, w_ada, b_ada)


def _mod_block(mod, k, tm, tn):
    nblk = mod.shape[2] // N_MOD // tn
    if mod.shape[1] == 1:
        return (1, 1, tn), lambda j, g, i: (g, 0, k * nblk + j)
    return (1, tm, tn), lambda j, g, i: (0, i, k * nblk + j)


def _mod_spec(mod, k, tm, d):
    shape, index = _mod_block(mod, k, tm, d)
    return pl.BlockSpec(shape, lambda g, i: index(0, g, i))


def _row_tile(r, target):
    t = min(r, target)
    while r % t:
        t //= 2
    return t


def _norm_mod_body(x_ref, g_ref, sc_ref, sh_ref, o_ref):
    y = _mean_sq_norm(x_ref[0]) * g_ref[...]
    o_ref[0] = (y * (1.0 + sc_ref[0]) + sh_ref[0]).astype(o_ref.dtype)


def _norm_mod(x3, g, mod, k_shift, k_scale):
    gg, r, d = x3.shape
    tm = _row_tile(r, 512)
    return pl.pallas_call(
        _norm_mod_body,
        grid=(gg, r // tm),
        in_specs=[pl.BlockSpec((1, tm, d), lambda b, i: (b, i, 0)),
                  pl.BlockSpec((1, d), lambda b, i: (0, 0)),
                  _mod_spec(mod, k_scale, tm, d),
                  _mod_spec(mod, k_shift, tm, d)],
        out_specs=pl.BlockSpec((1, tm, d), lambda b, i: (b, i, 0)),
        out_shape=jax.ShapeDtypeStruct((gg, r, d), BF16),
        compiler_params=_cparams("arbitrary", "arbitrary"),
        name="norm_mod",
    )(x3, g, mod, mod)


def _mm_body(*refs, epilogue, kinds):
    a_ref, w_ref = refs[0], refs[1]
    o_ref = refs[-1]
    acc = jnp.dot(a_ref[0], w_ref[...], preferred_element_type=F32)
    ex = [r[...] if kd == "bias" else r[0] for r, kd in zip(refs[2:-1], kinds)]
    o_ref[0] = epilogue(acc, *ex).astype(o_ref.dtype)


def _matmul(a3, w, epilogue, extras=(), out_dtype=F32, cols=None, name="mm"):
    gg, r, k = a3.shape
    c0, n = cols if cols is not None else (0, w.shape[1])
    tm = _row_tile(r, 1024)
    tn = _row_tile(math.gcd(n, c0), FEATURE_TILE)
    tile_bytes = sum(e[0].dtype.itemsize for e in extras if e[1] == "tile") + jnp.dtype(out_dtype).itemsize

    def vmem_bytes(tm_, tn_):
        return 2 * (tm_ * k * 2 + k * tn_ * 2 + tm_ * tn_ * tile_bytes) + 2 * tm_ * tn_ * 4

    while vmem_bytes(tm, tn) > MATMUL_VMEM_BUDGET and (tm > 256 or tn > 256):
        if (tm >= tn and tm > 256) or tn <= 256:
            tm //= 2
        else:
            tn //= 2
    assert c0 % tn == 0
    in_specs = [pl.BlockSpec((1, tm, k), lambda j, g, i: (g, i, 0)),
                pl.BlockSpec((k, tn), lambda j, g, i: (0, c0 // tn + j))]
    args = [a3, w]
    kinds = []
    for e in extras:
        arr, kind = e[0], e[1]
        kinds.append(kind)
        args.append(arr)
        if kind == "tile":
            off = e[2] if len(e) > 2 else 0
            nb = n // tn
            in_specs.append(pl.BlockSpec((1, tm, tn), lambda j, g, i, off=off, nb=nb: (g, i, off * nb + j)))
        elif kind == "bias":
            in_specs.append(pl.BlockSpec((1, tn), lambda j, g, i: (0, j)))
        else:
            in_specs.append(pl.BlockSpec(*_mod_block(arr, e[2], tm, tn)))
    return pl.pallas_call(
        functools.partial(_mm_body, epilogue=epilogue, kinds=tuple(kinds)),
        grid=(n // tn, gg, r // tm),
        in_specs=in_specs,
        out_specs=pl.BlockSpec((1, tm, tn), lambda j, g, i: (g, i, j)),
        out_shape=jax.ShapeDtypeStruct((gg, r, n), out_dtype),
        compiler_params=_cparams("arbitrary", "arbitrary", "arbitrary"),
        name=name,
    )(*args)


def _ep_none(acc):
    return acc


def _ep_gelu(acc):
    return _gelu(acc)


def _ep_sigmoid(acc):
    return jax.nn.sigmoid(acc)


def _ep_softplus_bias(acc, bias):
    return _softplus(acc + bias)


def _ep_merge(acc, y_a, sg_a, sg_b):
    return sg_a * y_a + sg_b * acc


def _ep_residual(acc, x, gate):
    return x + gate * acc


def _conv_prompt_body(x_ref, w_ref, b_ref, o_ref, pad_ref, *, width):
    ln = x_ref.shape[1]
    pad_ref[0:SUBLANES, :] = jnp.zeros((SUBLANES, pad_ref.shape[1]), F32)
    pad_ref[SUBLANES:SUBLANES + ln, :] = x_ref[0]
    acc = b_ref[...] + x_ref[0] * w_ref[width - 1:width, :]
    for k in range(width - 1):
        s = SUBLANES - (width - 1) + k
        acc = acc + pad_ref[s:s + ln, :] * w_ref[k:k + 1, :]
    o_ref[0] = _silu(acc)


def _conv_prompt(x3, conv_w, conv_b):
    b, ln, c = x3.shape
    width = conv_w.shape[0]
    ct = _row_tile(c, 512)
    return pl.pallas_call(
        functools.partial(_conv_prompt_body, width=width),
        grid=(b, c // ct),
        in_specs=[pl.BlockSpec((1, ln, ct), lambda i, j: (i, 0, j)),
                  pl.BlockSpec((width, ct), lambda i, j: (0, j)),
                  pl.BlockSpec((1, ct), lambda i, j: (0, j))],
        out_specs=pl.BlockSpec((1, ln, ct), lambda i, j: (i, 0, j)),
        out_shape=jax.ShapeDtypeStruct((b, ln, c), F32),
        scratch_shapes=[pltpu.VMEM((ln + SUBLANES, ct), F32)],
        compiler_params=_cparams("arbitrary", "arbitrary"),
        name="conv_prompt",
    )(x3, conv_w, conv_b)


def _conv_slab_body(xp_ref, w_ref, b_ref, o_ref, *, width):
    for t in range(o_ref.shape[0]):
        acc = b_ref[...] + xp_ref[t] * w_ref[0:1, :]
        for k in range(1, width):
            acc = acc + xp_ref[t + k] * w_ref[k:k + 1, :]
        o_ref[t] = _silu(acc)


def _conv_slab(xp, conv_w, conv_b):
    lp, b, c = xp.shape
    width = conv_w.shape[0]
    ln = lp - width + 1
    ct = _row_tile(c, 1024)
    return pl.pallas_call(
        functools.partial(_conv_slab_body, width=width),
        grid=(c // ct,),
        in_specs=[pl.BlockSpec((lp, b, ct), lambda j: (0, 0, j)),
                  pl.BlockSpec((width, ct), lambda j: (0, j)),
                  pl.BlockSpec((1, ct), lambda j: (0, j))],
        out_specs=pl.BlockSpec((ln, b, ct), lambda j: (0, 0, j)),
        out_shape=jax.ShapeDtypeStruct((ln, b, c), F32),
        compiler_params=_cparams("arbitrary"),
        name="conv_sample",
    )(xp, conv_w, conv_b)


def _ssd_prompt_body(x_ref, b_ref, c_ref, dt_ref, z_ref, alog_ref, dsk_ref, ng_ref, y_ref, hl_ref,
                     h_scr, y_scr, *, headdim):
    ci = pl.program_id(2)

    @pl.when(ci == 0)
    def _():
        h_scr[...] = jnp.zeros_like(h_scr)

    x = x_ref[0]
    cl, gw = x.shape
    dt = dt_ref[0]
    a = -jnp.exp(alog_ref[...])
    row = lax.broadcasted_iota(jnp.int32, (cl, cl), 0)
    col = lax.broadcasted_iota(jnp.int32, (cl, cl), 1)
    tril = row >= col
    acum = jnp.dot(tril.astype(F32), dt * a, precision=_HIGHEST, preferred_element_type=F32)
    acum_t = acum.T
    lane_head = lax.broadcasted_iota(jnp.int32, (cl, LANES), 1) // headdim
    heads_per_blk = LANES // headdim
    bb = b_ref[0].astype(BF16)
    cc = c_ref[0].astype(BF16)
    cb = lax.dot_general(cc, bb, _NT, preferred_element_type=F32)
    h_t = h_scr[...]
    y_off = jnp.dot(cc, h_t.astype(BF16), preferred_element_type=F32)
    for j in range(gw // LANES):
        sl = slice(j * LANES, (j + 1) * LANES)
        h0 = j * heads_per_blk
        dte = jnp.broadcast_to(dt[:, h0:h0 + 1], (cl, LANES))
        ace = jnp.broadcast_to(acum[:, h0:h0 + 1], (cl, LANES))
        for q in range(1, heads_per_blk):
            dte = jnp.where(lane_head == q, dt[:, h0 + q:h0 + q + 1], dte)
            ace = jnp.where(lane_head == q, acum[:, h0 + q:h0 + q + 1], ace)
        xj = x[:, sl]
        xdt = xj * dte
        last = ace[cl - 1:cl, :]
        yd = jnp.zeros((cl, LANES), F32)
        for q in range(heads_per_blk):
            hh = h0 + q
            seg = acum[:, hh:hh + 1] - acum_t[hh:hh + 1, :]
            m = (cb * jnp.exp(jnp.where(tril, seg, -jnp.inf))).astype(BF16)
            xm = jnp.where(lane_head == q, xdt, 0.0).astype(BF16)
            yd = yd + jnp.dot(m, xm, preferred_element_type=F32)
        y = yd + y_off[:, sl] * jnp.exp(ace) + dsk_ref[:, sl] * xj
        y_scr[:, sl] = y * _silu(z_ref[0][:, sl])
        st_t = lax.dot_general(bb, (xdt * jnp.exp(last - ace)).astype(BF16), _TN,
                               preferred_element_type=F32)
        h_scr[:, sl] = h_t[:, sl] * jnp.exp(last) + st_t
    y_ref[0] = (_mean_sq_norm(y_scr[...]) * ng_ref[...]).astype(y_ref.dtype)

    @pl.when(ci == pl.num_programs(2) - 1)
    def _():
        hl_ref[0] = h_scr[...].T


def _ssd_prompt(xbc_act, dt_g, z, alog_g, dsk_e, norm_g, *, d_inner, groups, d_state, headdim):
    b, ln, _ = xbc_act.shape
    gw = d_inner // groups
    cl = min(SSD_CHUNK, ln)
    nc = ln // cl
    xb = d_inner // d_state
    return pl.pallas_call(
        functools.partial(_ssd_prompt_body, headdim=headdim),
        grid=(b, groups, nc),
        in_specs=[pl.BlockSpec((1, cl, gw), lambda i, g, c: (i, c, g)),
                  pl.BlockSpec((1, cl, d_state), lambda i, g, c: (i, c, xb + g)),
                  pl.BlockSpec((1, cl, d_state), lambda i, g, c: (i, c, xb + groups + g)),
                  pl.BlockSpec((1, cl, LANES), lambda i, g, c: (i, c, g)),
                  pl.BlockSpec((1, cl, gw), lambda i, g, c: (i, c, g)),
                  pl.BlockSpec((1, LANES), lambda i, g, c: (0, g)),
                  pl.BlockSpec((1, gw), lambda i, g, c: (0, g)),
                  pl.BlockSpec((1, gw), lambda i, g, c: (0, g))],
        out_specs=[pl.BlockSpec((1, cl, gw), lambda i, g, c: (i, c, g)),
                   pl.BlockSpec((1, gw, d_state), lambda i, g, c: (i, g, 0))],
        out_shape=[jax.ShapeDtypeStruct((b, ln, d_inner), BF16),
                   jax.ShapeDtypeStruct((b, d_inner, d_state), F32)],
        scratch_shapes=[pltpu.VMEM((d_state, gw), F32), pltpu.VMEM((cl, gw), F32)],
        compiler_params=_cparams("arbitrary", "arbitrary", "arbitrary"),
        name="ssd_prompt",
    )(xbc_act, xbc_act, xbc_act, dt_g, z, alog_g, dsk_e, norm_g)


def _ssd_sample_body(x_ref, dte_ref, b_ref, c_ref, z_ref, ae_ref, dsk_ref, ng_ref, h0_ref, y_ref, hn_ref,
                     *, groups, valid):
    x = x_ref[0]
    rows, di = x.shape
    gw = di // groups
    n = h0_ref.shape[2]
    rix = lax.broadcasted_iota(jnp.int32, (rows, di), 0)
    dte = jnp.where(rix < valid, dte_ref[0], 0.0)
    ac = dte * ae_ref[...]
    for sh in (1, 2, 4):
        ac = ac + jnp.where(rix >= sh, pltpu.roll(ac, sh, axis=0), 0.0)
    xdt = x * dte
    last = ac[rows - 1:rows, :]
    cdec = jnp.exp(last)
    hi = cdec.astype(BF16)
    r1 = cdec - hi.astype(F32)
    mid = r1.astype(BF16)
    lo = (r1 - mid.astype(F32)).astype(BF16)
    lhs = (xdt * jnp.exp(last - ac)).astype(BF16)
    lhs = jnp.where(rix == valid, hi, jnp.where(rix == valid + 1, mid, jnp.where(rix == valid + 2, lo, lhs)))
    rn = lax.broadcasted_iota(jnp.int32, (rows, n), 0)
    ones_rows = jnp.where((rn >= valid) & (rn < valid + 3), 1.0, 0.0).astype(BF16)
    rg = lax.broadcasted_iota(jnp.int32, (rows, gw), 0)
    eac = jnp.exp(ac)
    for g in range(groups):
        sl = slice(g * gw, (g + 1) * gw)
        bg = b_ref[0][:, g * n:(g + 1) * n]
        cg = c_ref[0][:, g * n:(g + 1) * n]
        h0 = h0_ref[0, sl, :]
        y_off = lax.dot_general(cg.astype(BF16), h0.astype(BF16), _NT, preferred_element_type=F32)
        rhs = jnp.concatenate([jnp.where(rn < valid, bg, 0.0).astype(BF16), ones_rows], axis=1)
        res = lax.dot_general(lhs[:, sl], rhs, _TN, preferred_element_type=F32)
        hn_ref[0, sl, :] = h0 * res[:, n:] + res[:, :n]
        acg = ac[:, sl]
        xdtg = xdt[:, sl]
        yd = jnp.zeros((rows, gw), F32)
        for s in range(valid):
            cbs = jnp.sum(cg * bg[s:s + 1, :], axis=-1, keepdims=True)
            w = jnp.where(rg >= s, jnp.exp(acg - acg[s:s + 1, :]), 0.0)
            yd = yd + cbs * w * xdtg[s:s + 1, :]
        xg = x[:, sl]
        y = (yd + y_off * eac[:, sl] + dsk_ref[:, sl] * xg) * _silu(z_ref[0][:, sl])
        y_ref[0, :, sl] = (_mean_sq_norm(y) * ng_ref[:, sl]).astype(y_ref.dtype)


def _ssd_sample(xbc_b, dte_b, z_b, a_e, dsk_e, norm_g, h0, *, groups, valid):
    b, rows, _ = xbc_b.shape
    _, di, n = h0.shape
    gn = groups * n
    assert valid + 3 <= rows and di % gn == 0
    seq = lambda i: (i, 0, 0)
    par = lambda i: (0, 0)
    return pl.pallas_call(
        functools.partial(_ssd_sample_body, groups=groups, valid=valid),
        grid=(b,),
        in_specs=[pl.BlockSpec((1, rows, di), seq),
                  pl.BlockSpec((1, rows, di), seq),
                  pl.BlockSpec((1, rows, gn), lambda i: (i, 0, di // gn)),
                  pl.BlockSpec((1, rows, gn), lambda i: (i, 0, di // gn + 1)),
                  pl.BlockSpec((1, rows, di), seq),
                  pl.BlockSpec((1, di), par),
                  pl.BlockSpec((1, di), par),
                  pl.BlockSpec((1, di), par),
                  pl.BlockSpec((1, di, n), seq)],
        out_specs=[pl.BlockSpec((1, rows, di), seq),
                   pl.BlockSpec((1, di, n), seq)],
        out_shape=[jax.ShapeDtypeStruct((b, rows, di), F32),
                   jax.ShapeDtypeStruct((b, di, n), F32)],
        compiler_params=_cparams("arbitrary"),
        name="ssd_sample",
    )(xbc_b, dte_b, xbc_b, xbc_b, z_b, a_e, dsk_e, norm_g, h0)


def _layernorm(v, g, b):
    mu = jnp.mean(v, axis=-1, keepdims=True)
    d = v - mu
    var = jnp.mean(d * d, axis=-1, keepdims=True)
    return d * lax.rsqrt(var + EPS) * g + b


def _gmlp_prompt_body(gv_ref, gu_ref, lg_ref, lb_ref, ws_ref, bst_ref, o_ref):
    vn = _layernorm(gv_ref[0], lg_ref[...], lb_ref[...])
    cl, d = vn.shape
    groups = ws_ref.shape[0]
    gw = d // groups
    row = lax.broadcasted_iota(jnp.int32, (cl, cl), 0)
    col = lax.broadcasted_iota(jnp.int32, (cl, cl), 1)
    tril = row >= col
    for g in range(groups):
        sl = slice(g * gw, (g + 1) * gw)
        w = jnp.where(tril, ws_ref[g], 0.0).astype(BF16)
        s = jnp.dot(w, vn[:, sl].astype(BF16), preferred_element_type=F32) + bst_ref[:, g:g + 1]
        o_ref[0, :, sl] = (gu_ref[0][:, sl] * s).astype(o_ref.dtype)


def _gmlp_prompt(guv, ln_g, ln_b, ws, bs_t):
    b, ln, d2 = guv.shape
    d = d2 // 2
    cl = min(GMLP_CHUNK, ln)
    groups = ws.shape[0]
    return pl.pallas_call(
        _gmlp_prompt_body,
        grid=(b, ln // cl),
        in_specs=[pl.BlockSpec((1, cl, d), lambda i, c: (i, c, 1)),
                  pl.BlockSpec((1, cl, d), lambda i, c: (i, c, 0)),
                  pl.BlockSpec((1, d), lambda i, c: (0, 0)),
                  pl.BlockSpec((1, d), lambda i, c: (0, 0)),
                  pl.BlockSpec((groups, cl, cl), lambda i, c: (0, 0, 0)),
                  pl.BlockSpec((cl, groups), lambda i, c: (0, 0))],
        out_specs=pl.BlockSpec((1, cl, d), lambda i, c: (i, c, 0)),
        out_shape=jax.ShapeDtypeStruct((b, ln, d), BF16),
        compiler_params=_cparams("arbitrary", "arbitrary"),
        name="gmlp_prompt",
    )(guv, guv, ln_g, ln_b, ws, bs_t)


def _gmlp_slab_body(gv_ref, gu_ref, lg_ref, lb_ref, wrow_ref, brow_ref, o_ref, vn_ref):
    ln = gv_ref.shape[0]
    vn = [_layernorm(gv_ref[t], lg_ref[...], lb_ref[...]) for t in range(ln)]
    for t in range(ln):
        vn_ref[t] = vn[t]
        s = brow_ref[t:t + 1, :] + vn[0] * wrow_ref[t * ln:t * ln + 1, :]
        for k in range(1, t + 1):
            s = s + vn[k] * wrow_ref[t * ln + k:t * ln + k + 1, :]
        o_ref[t] = (gu_ref[t] * s).astype(o_ref.dtype)


def _gmlp_slab(guv, ln_g, ln_b, wrow, brow):
    ln, b, d2 = guv.shape
    d = d2 // 2
    bt = _row_tile(b, 32)
    return pl.pallas_call(
        _gmlp_slab_body,
        grid=(b // bt,),
        in_specs=[pl.BlockSpec((ln, bt, d), lambda i: (0, i, 1)),
                  pl.BlockSpec((ln, bt, d), lambda i: (0, i, 0)),
                  pl.BlockSpec((1, d), lambda i: (0, 0)),
                  pl.BlockSpec((1, d), lambda i: (0, 0)),
                  pl.BlockSpec((ln * ln, d), lambda i: (0, 0)),
                  pl.BlockSpec((ln, d), lambda i: (0, 0))],
        out_specs=[pl.BlockSpec((ln, bt, d), lambda i: (0, i, 0)),
                   pl.BlockSpec((ln, bt, d), lambda i: (0, i, 0))],
        out_shape=[jax.ShapeDtypeStruct((ln, b, d), BF16),
                   jax.ShapeDtypeStruct((ln, b, d), F32)],
        compiler_params=_cparams("arbitrary"),
        name="gmlp_sample",
    )(guv, guv, ln_g, ln_b, wrow, brow)


def _peer_scores_body(h_ref, wq_ref, k_ref, o_ref):
    q = jnp.dot(h_ref[...], wq_ref[...], preferred_element_type=F32).astype(BF16)
    hb, _, _, half = k_ref.shape
    for hh in range(hb):
        for i in range(2):
            c0 = (hh * 2 + i) * half
            o_ref[hh, i] = lax.dot_general(k_ref[hh, i], q[:, c0:c0 + half], _NT, preferred_element_type=F32)


def _peer_scores(h2, wq, keys, tt):
    m, d = h2.shape
    heads, _, nk, half = keys.shape
    hb = _row_tile(heads, 4)
    return pl.pallas_call(
        _peer_scores_body,
        grid=(heads // hb, m // tt),
        in_specs=[pl.BlockSpec((tt, d), lambda h, t: (t, 0)),
                  pl.BlockSpec((d, hb * 2 * half), lambda h, t: (0, h)),
                  pl.BlockSpec((hb, 2, nk, half), lambda h, t: (h, 0, 0, 0))],
        out_specs=pl.BlockSpec((hb, 2, nk, tt), lambda h, t: (h, 0, 0, t)),
        out_shape=jax.ShapeDtypeStruct((heads, 2, nk, m), F32),
        compiler_params=_cparams("arbitrary", "arbitrary"),
        name="peer_scores",
    )(h2, wq, keys)


def _top_values(s, k):
    n = s.shape[0]
    ridx = lax.broadcasted_iota(jnp.int32, s.shape, 0)
    kidx = lax.broadcasted_iota(jnp.int32, (k, s.shape[1]), 0)
    out = jnp.zeros((k, s.shape[1]), F32)
    for r in range(k):
        m = jnp.max(s, axis=0, keepdims=True)
        first = jnp.min(jnp.where(s == m, ridx, n), axis=0, keepdims=True)
        s = jnp.where(ridx == first, -jnp.inf, s)
        out = jnp.where(kidx == r, m, out)
    return out


def _sorted_top_values(s, k):
    n, t = s.shape
    assert n == SUBLANES * k and k & (k - 1) == 0
    x = [s[r * SUBLANES:(r + 1) * SUBLANES, :] for r in range(k)]

    def merge_descending(c):
        stride = k // 2
        while stride >= 1:
            for i in range(k):
                l = i ^ stride
                if l > i:
                    c[i], c[l] = jnp.maximum(c[i], c[l]), jnp.minimum(c[i], c[l])
            stride //= 2
        return c

    size = 2
    while size <= k:
        stride = size // 2
        while stride >= 1:
            for i in range(k):
                l = i ^ stride
                if l > i:
                    hi, lo = jnp.maximum(x[i], x[l]), jnp.minimum(x[i], x[l])
                    x[i], x[l] = (hi, lo) if (i & size) == 0 else (lo, hi)
            stride //= 2
        size *= 2
    shift = SUBLANES // 2
    while shift >= 1:
        y = [pltpu.roll(a, shift, axis=0) for a in x]
        x = merge_descending([jnp.maximum(x[i], y[k - 1 - i]) for i in range(k)])
        shift //= 2
    row = lax.broadcasted_iota(jnp.int32, (SUBLANES, t), 0)
    halves = []
    for h0 in range(0, k, SUBLANES):
        acc = x[h0]
        for r in range(1, SUBLANES):
            acc = jnp.where(row == r, x[h0 + r], acc)
        halves.append(acc)
    return jnp.concatenate(halves, axis=0) if len(halves) > 1 else halves[0]


def _pair_candidates(v1, v2, k):
    isplit = 4
    lanes = v1.shape[1]

    def rows(nr):
        return lax.broadcasted_iota(jnp.int32, (nr, lanes), 0)

    pieces = []
    for i in range(min(isplit, k)):
        nj = k // (i + 1)
        nr = -(-nj // SUBLANES) * SUBLANES
        p = v1[i:i + 1, :] + v2[0:nr, :]
        pieces.append(p if nr == nj else jnp.where(rows(nr) < nj, p, -jnp.inf))
    for j in range(k // (isplit + 1)):
        ni = k // (j + 1)
        nr = -(-ni // SUBLANES) * SUBLANES
        r = rows(nr)
        pieces.append(jnp.where((r >= isplit) & (r < ni), v1[0:nr, :] + v2[j:j + 1, :], -jnp.inf))
    return jnp.concatenate(pieces, axis=0)


def _peer_stats_body(s_ref, o_ref, *, topk):
    heads, _, nk, _ = s_ref.shape
    sub_key_top = _sorted_top_values if nk == SUBLANES * topk else _top_values
    for h in range(heads):
        v1 = sub_key_top(s_ref[h, 0], topk)
        v2 = sub_key_top(s_ref[h, 1], topk)
        top = _top_values(_pair_candidates(v1, v2, topk), topk)
        mx = top[0:1, :]
        lse = mx + jnp.log(jnp.sum(jnp.exp(top - mx), axis=0, keepdims=True))
        o_ref[h, 0:1, :] = top[topk - 1:topk, :]
        o_ref[h, 1:2, :] = lse


def _peer_stats(scores, tt):
    heads, _, nk, m = scores.shape
    return pl.pallas_call(
        functools.partial(_peer_stats_body, topk=PEER_TOPK),
        grid=(m // tt,),
        in_specs=[pl.BlockSpec((heads, 2, nk, tt), lambda t: (0, 0, 0, t))],
        out_specs=pl.BlockSpec((heads, PEER_STATS, tt), lambda t: (0, 0, t)),
        out_shape=jax.ShapeDtypeStruct((heads, PEER_STATS, m), F32),
        compiler_params=_cparams("arbitrary"),
        name="peer_stats",
    )(scores)


def _peer_mix_body(h_ref, u_ref, v_ref, s_ref, st_ref, o_ref, *, nk):
    e = pl.program_id(1)

    @pl.when(e == 0)
    def _():
        o_ref[...] = jnp.zeros_like(o_ref)

    et = u_ref.shape[0]
    heads = s_ref.shape[0]
    act = _gelu(lax.dot_general(u_ref[...], h_ref[...], _NT, preferred_element_type=F32))
    parts = []
    for ii in range(et // nk):
        i = e * (et // nk) + ii
        wsum = jnp.zeros((nk, act.shape[1]), F32)
        for h in range(heads):
            c = s_ref[h, 0, pl.ds(i, 1), :] + s_ref[h, 1]
            wsum = wsum + jnp.where(c >= st_ref[h, 0:1, :], jnp.exp(c - st_ref[h, 1:2, :]), 0.0)
        parts.append((wsum * act[ii * nk:(ii + 1) * nk, :]).astype(BF16))
    wg = jnp.concatenate(parts, axis=0) if len(parts) > 1 else parts[0]
    o_ref[...] += lax.dot_general(wg, v_ref[...].astype(BF16), _TN, preferred_element_type=F32)


def _peer_mix(h2, u, v, scores, stats, tt, et):
    m, d = h2.shape
    ne = u.shape[0]
    heads, _, nk, _ = scores.shape
    return pl.pallas_call(
        functools.partial(_peer_mix_body, nk=nk),
        grid=(m // tt, ne // et),
        in_specs=[pl.BlockSpec((tt, d), lambda t, e: (t, 0)),
                  pl.BlockSpec((et, d), lambda t, e: (e, 0)),
                  pl.BlockSpec((et, d), lambda t, e: (e, 0)),
                  pl.BlockSpec((heads, 2, nk, tt), lambda t, e: (0, 0, 0, t)),
                  pl.BlockSpec((heads, PEER_STATS, tt), lambda t, e: (0, 0, t))],
        out_specs=pl.BlockSpec((tt, d), lambda t, e: (t, 0)),
        out_shape=jax.ShapeDtypeStruct((m, d), F32),
        compiler_params=_cparams("arbitrary", "arbitrary"),
        name="peer_mix",
    )(h2, u, v, scores, stats)


def _final_body(x_ref, p_ref, gate_ref, fg_ref, o_ref):
    xn = x_ref[0] + gate_ref[0] * p_ref[...]
    o_ref[0] = _mean_sq_norm(xn) * fg_ref[...]


def _final(x3, peer, row0, mod, k_gate, final_g):
    gg, r, d = x3.shape
    tm = _row_tile(r, 256)
    nb = r // tm
    r0 = row0 // tm
    return pl.pallas_call(
        _final_body,
        grid=(gg, nb),
        in_specs=[pl.BlockSpec((1, tm, d), lambda b, i: (b, i, 0)),
                  pl.BlockSpec((tm, d), lambda b, i: (r0 + b * nb + i, 0)),
                  _mod_spec(mod, k_gate, tm, d),
                  pl.BlockSpec((1, d), lambda b, i: (0, 0))],
        out_specs=pl.BlockSpec((1, tm, d), lambda b, i: (b, i, 0)),
        out_shape=jax.ShapeDtypeStruct((gg, r, d), F32),
        compiler_params=_cparams("arbitrary", "arbitrary"),
        name="final",
    )(x3, peer, mod, final_g)


def _mixer(x3, mod, w, dims, *, sample, ssm0=None, conv0=None):
    d, d_inner, groups, d_state, headdim, conv_dim = dims
    heads = d_inner // headdim
    hpg = heads // groups
    h = _norm_mod(x3, w["norm1_g"], mod, 0, 1)
    w_in, cz, cxbc, cdt, cuv, cg = w["w_in"]
    z = _matmul(h, w_in, _ep_none, cols=cz, name="in_z")
    xbc = _matmul(h, w_in, _ep_none, cols=cxbc, name="in_xbc")
    dt = _matmul(h, w_in, _ep_softplus_bias, extras=[(w["dt_bias"], "bias")], cols=cdt, name="in_dt")
    dt = dt[:, :, :heads]
    guv = _matmul(h, w_in, _ep_gelu, cols=cuv, name="in_uv")
    sg = _matmul(h, w_in, _ep_sigmoid, cols=cg, name="in_gates")

    if not sample:
        b, ln, _ = x3.shape
        xbc_act = _conv_prompt(xbc, w["conv_w"], w["conv_b"])
        conv_new = xbc[:, ln - (w["conv_w"].shape[0] - 1):, :]
        dt_g = jnp.pad(dt.reshape(b, ln, groups, hpg), ((0, 0), (0, 0), (0, 0), (0, LANES - hpg)))
        dt_g = dt_g.reshape(b, ln, groups * LANES)
        yn, h_new = _ssd_prompt(xbc_act, dt_g, z, w["alog_g"], w["dsk_e"], w["ssd_norm_g"],
                                d_inner=d_inner, groups=groups, d_state=d_state, headdim=headdim)
        ssm_new = h_new.reshape(b, heads, headdim, d_state)
        yb = _gmlp_prompt(guv, w["gmlp_ln_g"], w["gmlp_ln_b"], w["gmlp_ws"], w["gmlp_bs_t"])
        v_rows = None
    else:
        b = conv0.shape[0]
        ln = x3.shape[1] // b
        cw = w["conv_w"].shape[0]

        def slabs(t):
            return t.reshape(ln, b, t.shape[-1])

        def flat(t):
            return t.reshape(1, ln * b, t.shape[-1])

        xp = jnp.concatenate([jnp.transpose(conv0, (1, 0, 2)), slabs(xbc)], axis=0)
        xbc_act = _conv_slab(xp, w["conv_w"], w["conv_b"])
        conv_new = jnp.transpose(xp[ln:ln + cw - 1], (1, 0, 2))

        def to_rows(t):
            return jnp.pad(jnp.transpose(t, (1, 0, 2)), ((0, 0), (0, SUBLANES - ln), (0, 0)))

        dte_b = jnp.repeat(to_rows(slabs(dt)), headdim, axis=2)
        yn_b, h_new = _ssd_sample(to_rows(xbc_act), dte_b, to_rows(slabs(z)), w["a_e"], w["dsk_e"],
                                  w["ssd_norm_g"], ssm0.reshape(b, d_inner, d_state), groups=groups, valid=ln)
        ssm_new = h_new.reshape(b, heads, headdim, d_state)
        yn = flat(jnp.transpose(yn_b[:, :ln], (1, 0, 2)).astype(BF16))
        wrow, brow = _slab_gate_rows(w, ln)
        yb, v_tm = _gmlp_slab(slabs(guv), w["gmlp_ln_g"], w["gmlp_ln_b"], wrow, brow)
        yb = flat(yb)
        v_rows = jnp.transpose(v_tm, (1, 0, 2))

    y_a = _matmul(yn, w["w_oa"], _ep_none, name="out_a")
    merged = _matmul(yb, w["w_ob"], _ep_merge, extras=[(y_a, "tile"), (sg, "tile", 0), (sg, "tile", 1)],
                     out_dtype=BF16, name="out_b_merge")
    x1 = _matmul(merged, w["w_out"], _ep_residual, extras=[(x3, "tile"), (mod, "mod", 2)], name="out_proj")
    h2 = _norm_mod(x1, w["norm2_g"], mod, 3, 4)
    return x1, h2, ssm_new, conv_new, v_rows


def _slab_gate_rows(w, ln):
    ws, bs = w["gmlp_ws"], w["gmlp_bs"]
    gw = w["gmlp_ln_g"].shape[1] // ws.shape[0]
    wt = jnp.where(jnp.tril(jnp.ones((ln, ln), bool)), ws[:, :ln, :ln], 0.0)
    wrow = jnp.repeat(jnp.transpose(wt, (1, 2, 0)).reshape(ln * ln, -1), gw, axis=1)
    brow = jnp.repeat(jnp.transpose(bs[:, :ln]), gw, axis=1)
    return wrow, brow


def kernel(x_prompt, x_sample, state_ssm, state_conv, c_prompt, c_sample, w_ada, b_ada, norm1_g, w_in, conv_w, conv_b, dt_bias, a_log, d_skip, ssd_norm_g, w_oa, gmlp_ln_g, gmlp_ln_b, gmlp_ws, gmlp_bs, w_ob, w_out, norm2_g, peer_wq, peer_keys, peer_u, peer_v, final_g):
    depth = w_ada.shape[0]
    bp, lp, d = x_prompt.shape
    bs_, ls, _ = x_sample.shape
    heads = a_log.shape[1]
    headdim, d_state = state_ssm.shape[3], state_ssm.shape[4]
    d_inner = heads * headdim
    conv_dim = conv_w.shape[2]
    groups = (conv_dim - d_inner) // (2 * d_state)
    hpg = heads // groups
    d_gmlp = gmlp_ln_g.shape[1]
    dims = (d, d_inner, groups, d_state, headdim, conv_dim)
    off_xbc = d_inner
    off_dt = off_xbc + conv_dim
    off_u = off_dt + heads
    off_ga = off_u + 2 * d_gmlp

    xp = x_prompt
    xs = jnp.transpose(x_sample, (1, 0, 2)).reshape(1, ls * bs_, d)
    n_c = bs_ + bp
    c_all = jnp.pad(jnp.concatenate([c_sample, c_prompt], axis=0), ((0, (-n_c) % SUBLANES), (0, 0)))
    ssm_p, conv_p, ssm_s, conv_s, v_s = [], [], [], [], []
    for l in range(depth):
        mod = _ada(c_all, w_ada[l], b_ada[l][None, :])
        mod_s = jnp.tile(mod[:bs_], (ls, 1))[None]
        mod_p = mod[bs_:n_c][:, None, :]
        wl = w_in[l]
        dt_w = heads + (-heads) % LANES
        gap = dt_w - heads + (-(off_dt + dt_w)) % FEATURE_TILE
        w_in_bf = jnp.concatenate([wl[:, :off_u].astype(BF16), jnp.zeros((d, gap), BF16), wl[:, off_u:].astype(BF16)],
                                  axis=1)
        a = -jnp.exp(a_log[l])
        w = dict(
            norm1_g=norm1_g[l][None, :], norm2_g=norm2_g[l][None, :],
            w_in=(w_in_bf, (0, d_inner), (off_xbc, conv_dim), (off_dt, dt_w), (off_u + gap, 2 * d_gmlp),
                  (off_ga + gap, 2 * d)),
            dt_bias=jnp.pad(dt_bias[l], (0, dt_w - heads))[None, :],
            conv_w=conv_w[l], conv_b=conv_b[l][None, :],
            alog_g=jnp.pad(a_log[l].reshape(groups, hpg), ((0, 0), (0, LANES - hpg))).reshape(1, groups * LANES),
            a_e=jnp.repeat(a, headdim)[None, :], dsk_e=jnp.repeat(d_skip[l], headdim)[None, :],
            ssd_norm_g=ssd_norm_g[l][None, :],
            w_oa=w_oa[l].astype(BF16), w_ob=w_ob[l].astype(BF16), w_out=w_out[l].astype(BF16),
            gmlp_ln_g=gmlp_ln_g[l][None, :], gmlp_ln_b=gmlp_ln_b[l][None, :],
            gmlp_ws=gmlp_ws[l], gmlp_bs=gmlp_bs[l], gmlp_bs_t=jnp.transpose(gmlp_bs[l]),
        )
        x1p, h2p, hp, cp, _ = _mixer(xp, mod_p, w, dims, sample=False)
        x1s, h2s, hs, cs, vs = _mixer(xs, mod_s, w, dims, sample=True, ssm0=state_ssm[l], conv0=state_conv[l])

        h2 = jnp.concatenate([h2p.reshape(bp * lp, d), h2s.reshape(ls * bs_, d)], axis=0)
        m = h2.shape[0]
        tt = _row_tile(m, 512)
        ne = peer_u.shape[1]
        et = _row_tile(ne, 256)
        scores = _peer_scores(h2, peer_wq[l].astype(BF16), peer_keys[l].astype(BF16), tt)
        stats = _peer_stats(scores, tt)
        peer_t = _peer_mix(h2, peer_u[l].astype(BF16), peer_v[l], scores, stats, tt, et)
        last = l == depth - 1
        fg = final_g[None, :] if last else jnp.ones((1, d), F32)
        assert last, "only DEPTH == 1 is supported"
        xp = _final(x1p, peer_t, 0, mod_p, 5, fg)
        xs = _final(x1s, peer_t, bp * lp, mod_s, 5, fg)
        ssm_p.append(hp)
        conv_p.append(cp)
        ssm_s.append(hs)
        conv_s.append(cs)
        v_s.append(vs)
    y_prompt = xp
    y_sample = jnp.transpose(xs.reshape(ls, bs_, d), (1, 0, 2))
    return (y_prompt, y_sample, jnp.stack(ssm_p), jnp.stack(conv_p), jnp.stack(ssm_s), jnp.stack(conv_s),
            jnp.stack(v_s))
```

```python
import functools
import math

import jax
import jax.numpy as jnp
from jax import lax
from jax.experimental import pallas as pl
from jax.experimental.pallas import tpu as pltpu

F32 = jnp.float32
BF16 = jnp.bfloat16
EPS = 1e-6
N_MOD = 6
PEER_TOPK = 16
PEER_STATS = 2
SSD_CHUNK = 128
GMLP_CHUNK = 128
SUBLANES = 8
LANES = 128
VMEM_LIMIT = 56 * 1024 * 1024
MATMUL_VMEM_BUDGET = 48 * 1024 * 1024
FEATURE_TILE = 1024
_SQRT_HALF = 0.7071067811865476
_HIGHEST = lax.Precision.HIGHEST
_NT = (((1,), (1,)), ((), ()))
_TN = (((0,), (0,)), ((), ()))


def _cparams(*sem):
    return pltpu.CompilerParams(dimension_semantics=sem, vmem_limit_bytes=VMEM_LIMIT)


def _silu(x):
    return x * jax.nn.sigmoid(x)


def _gelu(x):
    return 0.5 * x * (1.0 + lax.erf(x * _SQRT_HALF))


def _softplus(x):
    return jnp.maximum(x, 0.0) + jnp.log1p(jnp.exp(-jnp.abs(x)))


def _mean_sq_norm(x):
    return x * lax.rsqrt(jnp.mean(x * x, axis=-1, keepdims=True) + EPS)


def _ada_body(c_ref, w_ref, b_ref, o_ref):
    a = _silu(c_ref[...]).astype(BF16)
    o_ref[...] = jnp.dot(a, w_ref[...].astype(BF16), preferred_element_type=F32) + b_ref[...]


def _ada(c_all, w_ada, b_ada):
    m, d = c_all.shape
    n = w_ada.shape[1]
    tn = min(512, n)
    return pl.pallas_call(
        _ada_body,
        grid=(n // tn,),
        in_specs=[pl.BlockSpec((m, d), lambda j: (0, 0)),
                  pl.BlockSpec((d, tn), lambda j: (0, j)),
                  pl.BlockSpec((1, tn), lambda j: (0, j))],
        out_specs=pl.BlockSpec((m, tn), lambda j: (0, j)),
        out_shape=jax.ShapeDtypeStruct((m, n), F32),
        compiler_params=_cparams("arbitrary"),
        name="ada",
    )(c_all, w_ada, b_ada)


def _mod_block(mod, k, tm, tn):
    nblk = mod.shape[2] // N_MOD // tn
    if mod.shape[1] == 1:
        return (1, 1, tn), lambda j, g, i: (g, 0, k * nblk + j)
    return (1, tm, tn), lambda j, g, i: (0, i, k * nblk + j)


def _mod_spec(mod, k, tm, d):
    shape, index = _mod_block(mod, k, tm, d)
    return pl.BlockSpec(shape, lambda g, i: index(0, g, i))


def _row_tile(r, target):
    t = min(r, target)
    while r % t:
        t //= 2
    return t


def _norm_mod_body(x_ref, g_ref, sc_ref, sh_ref, o_ref):
    y = _mean_sq_norm(x_ref[0]) * g_ref[...]
    o_ref[0] = (y * (1.0 + sc_ref[0]) + sh_ref[0]).astype(o_ref.dtype)


def _norm_mod(x3, g, mod, k_shift, k_scale):
    gg, r, d = x3.shape
    tm = _row_tile(r, 512)
    return pl.pallas_call(
        _norm_mod_body,
        grid=(gg, r // tm),
        in_specs=[pl.BlockSpec((1, tm, d), lambda b, i: (b, i, 0)),
                  pl.BlockSpec((1, d), lambda b, i: (0, 0)),
                  _mod_spec(mod, k_scale, tm, d),
                  _mod_spec(mod, k_shift, tm, d)],
        out_specs=pl.BlockSpec((1, tm, d), lambda b, i: (b, i, 0)),
        out_shape=jax.ShapeDtypeStruct((gg, r, d), BF16),
        compiler_params=_cparams("arbitrary", "arbitrary"),
        name="norm_mod",
    )(x3, g, mod, mod)


def _mm_body(*refs, epilogue, kinds):
    a_ref, w_ref = refs[0], refs[1]
    o_ref = refs[-1]
    acc = jnp.dot(a_ref[0], w_ref[...], preferred_element_type=F32)
    ex = [r[...] if kd == "bias" else r[0] for r, kd in zip(refs[2:-1], kinds)]
    o_ref[0] = epilogue(acc, *ex).astype(o_ref.dtype)


def _matmul(a3, w, epilogue, extras=(), out_dtype=F32, name="mm"):
    gg, r, k = a3.shape
    w, (c0, n) = w if isinstance(w, tuple) else (w, (0, w.shape[1]))
    tm = _row_tile(r, 1024)
    tn = _row_tile(math.gcd(n, c0), FEATURE_TILE)
    tile_bytes = sum(e[0].dtype.itemsize for e in extras if e[1] == "tile") + jnp.dtype(out_dtype).itemsize

    def vmem_bytes(tm_, tn_):
        return 2 * (tm_ * k * 2 + k * tn_ * 2 + tm_ * tn_ * tile_bytes) + 2 * tm_ * tn_ * 4

    while vmem_bytes(tm, tn) > MATMUL_VMEM_BUDGET and (tm > 256 or tn > 256):
        if (tm >= tn and tm > 256) or tn <= 256:
            tm //= 2
        else:
            tn //= 2
    assert c0 % tn == 0
    in_specs = [pl.BlockSpec((1, tm, k), lambda j, g, i: (g, i, 0)),
                pl.BlockSpec((k, tn), lambda j, g, i: (0, c0 // tn + j))]
    args = [a3, w]
    kinds = []
    for e in extras:
        arr, kind = e[0], e[1]
        kinds.append(kind)
        args.append(arr)
        if kind == "tile":
            off = e[2] if len(e) > 2 else 0
            nb = n // tn
            in_specs.append(pl.BlockSpec((1, tm, tn), lambda j, g, i, off=off, nb=nb: (g, i, off * nb + j)))
        elif kind == "bias":
            in_specs.append(pl.BlockSpec((1, tn), lambda j, g, i: (0, j)))
        else:
            in_specs.append(pl.BlockSpec(*_mod_block(arr, e[2], tm, tn)))
    return pl.pallas_call(
        functools.partial(_mm_body, epilogue=epilogue, kinds=tuple(kinds)),
        grid=(n // tn, gg, r // tm),
        in_specs=in_specs,
        out_specs=pl.BlockSpec((1, tm, tn), lambda j, g, i: (g, i, j)),
        out_shape=jax.ShapeDtypeStruct((gg, r, n), out_dtype),
        compiler_params=_cparams("arbitrary", "arbitrary", "arbitrary"),
        name=name,
    )(*args)


def _ep_none(acc):
    return acc


def _ep_gelu(acc):
    return _gelu(acc)


def _ep_sigmoid(acc):
    return jax.nn.sigmoid(acc)


def _ep_softplus_bias(acc, bias):
    return _softplus(acc + bias)


def _ep_merge(acc, y_a, sg_a, sg_b):
    return sg_a * y_a + sg_b * acc


def _ep_residual(acc, x, gate):
    return x + gate * acc


def _conv_prompt_body(x_ref, w_ref, b_ref, o_ref, pad_ref, *, width):
    ln = x_ref.shape[1]
    pad_ref[0:SUBLANES, :] = jnp.zeros((SUBLANES, pad_ref.shape[1]), F32)
    pad_ref[SUBLANES:SUBLANES + ln, :] = x_ref[0]
    acc = b_ref[...] + x_ref[0] * w_ref[width - 1:width, :]
    for k in range(width - 1):
        s = SUBLANES - (width - 1) + k
        acc = acc + pad_ref[s:s + ln, :] * w_ref[k:k + 1, :]
    o_ref[0] = _silu(acc)


def _conv_prompt(x3, conv_w, conv_b):
    b, ln, c = x3.shape
    width = conv_w.shape[0]
    ct = _row_tile(c, 512)
    return pl.pallas_call(
        functools.partial(_conv_prompt_body, width=width),
        grid=(b, c // ct),
        in_specs=[pl.BlockSpec((1, ln, ct), lambda i, j: (i, 0, j)),
                  pl.BlockSpec((width, ct), lambda i, j: (0, j)),
                  pl.BlockSpec((1, ct), lambda i, j: (0, j))],
        out_specs=pl.BlockSpec((1, ln, ct), lambda i, j: (i, 0, j)),
        out_shape=jax.ShapeDtypeStruct((b, ln, c), F32),
        scratch_shapes=[pltpu.VMEM((ln + SUBLANES, ct), F32)],
        compiler_params=_cparams("arbitrary", "arbitrary"),
        name="conv_prompt",
    )(x3, conv_w, conv_b)


def _conv_slab_body(xp_ref, w_ref, b_ref, o_ref, *, width):
    for t in range(o_ref.shape[0]):
        acc = b_ref[...] + xp_ref[t] * w_ref[0:1, :]
        for k in range(1, width):
            acc = acc + xp_ref[t + k] * w_ref[k:k + 1, :]
        o_ref[t] = _silu(acc)


def _conv_slab(xp, conv_w, conv_b):
    lp, b, c = xp.shape
    width = conv_w.shape[0]
    ln = lp - width + 1
    ct = _row_tile(c, 1024)
    return pl.pallas_call(
        functools.partial(_conv_slab_body, width=width),
        grid=(c // ct,),
        in_specs=[pl.BlockSpec((lp, b, ct), lambda j: (0, 0, j)),
                  pl.BlockSpec((width, ct), lambda j: (0, j)),
                  pl.BlockSpec((1, ct), lambda j: (0, j))],
        out_specs=pl.BlockSpec((ln, b, ct), lambda j: (0, 0, j)),
        out_shape=jax.ShapeDtypeStruct((ln, b, c), F32),
        compiler_params=_cparams("arbitrary"),
        name="conv_sample",
    )(xp, conv_w, conv_b)


def _ssd_prompt_body(x_ref, b_ref, c_ref, dt_ref, z_ref, alog_ref, dsk_ref, ng_ref, y_ref, hl_ref,
                     h_scr, y_scr, *, headdim):
    ci = pl.program_id(2)

    @pl.when(ci == 0)
    def _():
        h_scr[...] = jnp.zeros_like(h_scr)

    x = x_ref[0]
    cl, gw = x.shape
    dt = dt_ref[0]
    a = -jnp.exp(alog_ref[...])
    row = lax.broadcasted_iota(jnp.int32, (cl, cl), 0)
    col = lax.broadcasted_iota(jnp.int32, (cl, cl), 1)
    tril = row >= col
    acum = jnp.dot(tril.astype(F32), dt * a, precision=_HIGHEST, preferred_element_type=F32)
    acum_t = acum.T
    lane_head = lax.broadcasted_iota(jnp.int32, (cl, LANES), 1) // headdim
    heads_per_blk = LANES // headdim
    bb = b_ref[0].astype(BF16)
    cc = c_ref[0].astype(BF16)
    cb = lax.dot_general(cc, bb, _NT, preferred_element_type=F32)
    h_t = h_scr[...]
    y_off = jnp.dot(cc, h_t.astype(BF16), preferred_element_type=F32)
    for j in range(gw // LANES):
        sl = slice(j * LANES, (j + 1) * LANES)
        h0 = j * heads_per_blk
        dte = jnp.broadcast_to(dt[:, h0:h0 + 1], (cl, LANES))
        ace = jnp.broadcast_to(acum[:, h0:h0 + 1], (cl, LANES))
        for q in range(1, heads_per_blk):
            dte = jnp.where(lane_head == q, dt[:, h0 + q:h0 + q + 1], dte)
            ace = jnp.where(lane_head == q, acum[:, h0 + q:h0 + q + 1], ace)
        xj = x[:, sl]
        xdt = xj * dte
        last = ace[cl - 1:cl, :]
        yd = jnp.zeros((cl, LANES), F32)
        for q in range(heads_per_blk):
            hh = h0 + q
            seg = acum[:, hh:hh + 1] - acum_t[hh:hh + 1, :]
            m = (cb * jnp.exp(jnp.where(tril, seg, -jnp.inf))).astype(BF16)
            xm = jnp.where(lane_head == q, xdt, 0.0).astype(BF16)
            yd = yd + jnp.dot(m, xm, preferred_element_type=F32)
        y = yd + y_off[:, sl] * jnp.exp(ace) + dsk_ref[:, sl] * xj
        y_scr[:, sl] = y * _silu(z_ref[0][:, sl])
        st_t = lax.dot_general(bb, (xdt * jnp.exp(last - ace)).astype(BF16), _TN,
                               preferred_element_type=F32)
        h_scr[:, sl] = h_t[:, sl] * jnp.exp(last) + st_t
    y_ref[0] = (_mean_sq_norm(y_scr[...]) * ng_ref[...]).astype(y_ref.dtype)

    @pl.when(ci == pl.num_programs(2) - 1)
    def _():
        hl_ref[0] = h_scr[...].T


def _ssd_prompt(xbc_act, dt_g, z, alog_g, dsk_e, norm_g, *, d_inner, groups, d_state, headdim):
    b, ln, _ = xbc_act.shape
    gw = d_inner // groups
    cl = min(SSD_CHUNK, ln)
    nc = ln // cl
    xb = d_inner // d_state
    return pl.pallas_call(
        functools.partial(_ssd_prompt_body, headdim=headdim),
        grid=(b, groups, nc),
        in_specs=[pl.BlockSpec((1, cl, gw), lambda i, g, c: (i, c, g)),
                  pl.BlockSpec((1, cl, d_state), lambda i, g, c: (i, c, xb + g)),
                  pl.BlockSpec((1, cl, d_state), lambda i, g, c: (i, c, xb + groups + g)),
                  pl.BlockSpec((1, cl, LANES), lambda i, g, c: (i, c, g)),
                  pl.BlockSpec((1, cl, gw), lambda i, g, c: (i, c, g)),
                  pl.BlockSpec((1, LANES), lambda i, g, c: (0, g)),
                  pl.BlockSpec((1, gw), lambda i, g, c: (0, g)),
                  pl.BlockSpec((1, gw), lambda i, g, c: (0, g))],
        out_specs=[pl.BlockSpec((1, cl, gw), lambda i, g, c: (i, c, g)),
                   pl.BlockSpec((1, gw, d_state), lambda i, g, c: (i, g, 0))],
        out_shape=[jax.ShapeDtypeStruct((b, ln, d_inner), BF16),
                   jax.ShapeDtypeStruct((b, d_inner, d_state), F32)],
        scratch_shapes=[pltpu.VMEM((d_state, gw), F32), pltpu.VMEM((cl, gw), F32)],
        compiler_params=_cparams("arbitrary", "arbitrary", "arbitrary"),
        name="ssd_prompt",
    )(xbc_act, xbc_act, xbc_act, dt_g, z, alog_g, dsk_e, norm_g)


def _ssd_sample_body(x_ref, dte_ref, b_ref, c_ref, z_ref, ae_ref, dsk_ref, ng_ref, h0_ref, y_ref, hn_ref,
                     *, groups, valid):
    x = x_ref[0]
    rows, di = x.shape
    gw = di // groups
    n = h0_ref.shape[2]
    rix = lax.broadcasted_iota(jnp.int32, (rows, di), 0)
    dte = jnp.where(rix < valid, dte_ref[0], 0.0)
    ac = dte * ae_ref[...]
    for sh in (1, 2, 4):
        ac = ac + jnp.where(rix >= sh, pltpu.roll(ac, sh, axis=0), 0.0)
    xdt = x * dte
    last = ac[rows - 1:rows, :]
    cdec = jnp.exp(last)
    hi = cdec.astype(BF16)
    r1 = cdec - hi.astype(F32)
    mid = r1.astype(BF16)
    lo = (r1 - mid.astype(F32)).astype(BF16)
    lhs = (xdt * jnp.exp(last - ac)).astype(BF16)
    lhs = jnp.where(rix == valid, hi, jnp.where(rix == valid + 1, mid, jnp.where(rix == valid + 2, lo, lhs)))
    rn = lax.broadcasted_iota(jnp.int32, (rows, n), 0)
    ones_rows = jnp.where((rn >= valid) & (rn < valid + 3), 1.0, 0.0).astype(BF16)
    rg = lax.broadcasted_iota(jnp.int32, (rows, gw), 0)
    eac = jnp.exp(ac)
    for g in range(groups):
        sl = slice(g * gw, (g + 1) * gw)
        bg = b_ref[0][:, g * n:(g + 1) * n]
        cg = c_ref[0][:, g * n:(g + 1) * n]
        h0 = h0_ref[0, sl, :]
        y_off = lax.dot_general(cg.astype(BF16), h0.astype(BF16), _NT, preferred_element_type=F32)
        rhs = jnp.concatenate([jnp.where(rn < valid, bg, 0.0).astype(BF16), ones_rows], axis=1)
        res = lax.dot_general(lhs[:, sl], rhs, _TN, preferred_element_type=F32)
        hn_ref[0, sl, :] = h0 * res[:, n:] + res[:, :n]
        acg = ac[:, sl]
        xdtg = xdt[:, sl]
        yd = jnp.zeros((rows, gw), F32)
        for s in range(valid):
            cbs = jnp.sum(cg * bg[s:s + 1, :], axis=-1, keepdims=True)
            w = jnp.where(rg >= s, jnp.exp(acg - acg[s:s + 1, :]), 0.0)
            yd = yd + cbs * w * xdtg[s:s + 1, :]
        xg = x[:, sl]
        y = (yd + y_off * eac[:, sl] + dsk_ref[:, sl] * xg) * _silu(z_ref[0][:, sl])
        y_ref[0, :, sl] = (_mean_sq_norm(y) * ng_ref[:, sl]).astype(y_ref.dtype)


def _ssd_sample(xbc_b, dte_b, z_b, a_e, dsk_e, norm_g, h0, *, groups, valid):
    b, rows, _ = xbc_b.shape
    _, di, n = h0.shape
    gn = groups * n
    assert valid + 3 <= rows and di % gn == 0
    seq = lambda i: (i, 0, 0)
    par = lambda i: (0, 0)
    return pl.pallas_call(
        functools.partial(_ssd_sample_body, groups=groups, valid=valid),
        grid=(b,),
        in_specs=[pl.BlockSpec((1, rows, di), seq),
                  pl.BlockSpec((1, rows, di), seq),
                  pl.BlockSpec((1, rows, gn), lambda i: (i, 0, di // gn)),
                  pl.BlockSpec((1, rows, gn), lambda i: (i, 0, di // gn + 1)),
                  pl.BlockSpec((1, rows, di), seq),
                  pl.BlockSpec((1, di), par),
                  pl.BlockSpec((1, di), par),
                  pl.BlockSpec((1, di), par),
                  pl.BlockSpec((1, di, n), seq)],
        out_specs=[pl.BlockSpec((1, rows, di), seq),
                   pl.BlockSpec((1, di, n), seq)],
        out_shape=[jax.ShapeDtypeStruct((b, rows, di), F32),
                   jax.ShapeDtypeStruct((b, di, n), F32)],
        compiler_params=_cparams("arbitrary"),
        name="ssd_sample",
    )(xbc_b, dte_b, xbc_b, xbc_b, z_b, a_e, dsk_e, norm_g, h0)


def _layernorm(v, g, b):
    mu = jnp.mean(v, axis=-1, keepdims=True)
    d = v - mu
    var = jnp.mean(d * d, axis=-1, keepdims=True)
    return d * lax.rsqrt(var + EPS) * g + b


def _gmlp_prompt_body(gv_ref, gu_ref, lg_ref, lb_ref, ws_ref, bst_ref, o_ref):
    vn = _layernorm(gv_ref[0], lg_ref[...], lb_ref[...])
    cl, d = vn.shape
    groups = ws_ref.shape[0]
    gw = d // groups
    row = lax.broadcasted_iota(jnp.int32, (cl, cl), 0)
    col = lax.broadcasted_iota(jnp.int32, (cl, cl), 1)
    tril = row >= col
    for g in range(groups):
        sl = slice(g * gw, (g + 1) * gw)
        w = jnp.where(tril, ws_ref[g], 0.0).astype(BF16)
        s = jnp.dot(w, vn[:, sl].astype(BF16), preferred_element_type=F32) + bst_ref[:, g:g + 1]
        o_ref[0, :, sl] = (gu_ref[0][:, sl] * s).astype(o_ref.dtype)


def _gmlp_prompt(guv, ln_g, ln_b, ws, bs_t):
    b, ln, d2 = guv.shape
    d = d2 // 2
    cl = min(GMLP_CHUNK, ln)
    groups = ws.shape[0]
    return pl.pallas_call(
        _gmlp_prompt_body,
        grid=(b, ln // cl),
        in_specs=[pl.BlockSpec((1, cl, d), lambda i, c: (i, c, 1)),
                  pl.BlockSpec((1, cl, d), lambda i, c: (i, c, 0)),
                  pl.BlockSpec((1, d), lambda i, c: (0, 0)),
                  pl.BlockSpec((1, d), lambda i, c: (0, 0)),
                  pl.BlockSpec((groups, cl, cl), lambda i, c: (0, 0, 0)),
                  pl.BlockSpec((cl, groups), lambda i, c: (0, 0))],
        out_specs=pl.BlockSpec((1, cl, d), lambda i, c: (i, c, 0)),
        out_shape=jax.ShapeDtypeStruct((b, ln, d), BF16),
        compiler_params=_cparams("arbitrary", "arbitrary"),
        name="gmlp_prompt",
    )(guv, guv, ln_g, ln_b, ws, bs_t)


def _gmlp_slab_body(gv_ref, gu_ref, lg_ref, lb_ref, wrow_ref, brow_ref, o_ref, vn_ref):
    ln = gv_ref.shape[0]
    vn = [_layernorm(gv_ref[t], lg_ref[...], lb_ref[...]) for t in range(ln)]
    for t in range(ln):
        vn_ref[t] = vn[t]
        s = brow_ref[t:t + 1, :] + vn[0] * wrow_ref[t * ln:t * ln + 1, :]
        for k in range(1, t + 1):
            s = s + vn[k] * wrow_ref[t * ln + k:t * ln + k + 1, :]
        o_ref[t] = (gu_ref[t] * s).astype(o_ref.dtype)


def _gmlp_slab(guv, ln_g, ln_b, wrow, brow):
    ln, b, d2 = guv.shape
    d = d2 // 2
    bt = _row_tile(b, 32)
    return pl.pallas_call(
        _gmlp_slab_body,
        grid=(b // bt,),
        in_specs=[pl.BlockSpec((ln, bt, d), lambda i: (0, i, 1)),
                  pl.BlockSpec((ln, bt, d), lambda i: (0, i, 0)),
                  pl.BlockSpec((1, d), lambda i: (0, 0)),
                  pl.BlockSpec((1, d), lambda i: (0, 0)),
                  pl.BlockSpec((ln * ln, d), lambda i: (0, 0)),
                  pl.BlockSpec((ln, d), lambda i: (0, 0))],
        out_specs=[pl.BlockSpec((ln, bt, d), lambda i: (0, i, 0)),
                   pl.BlockSpec((ln, bt, d), lambda i: (0, i, 0))],
        out_shape=[jax.ShapeDtypeStruct((ln, b, d), BF16),
                   jax.ShapeDtypeStruct((ln, b, d), F32)],
        compiler_params=_cparams("arbitrary"),
        name="gmlp_sample",
    )(guv, guv, ln_g, ln_b, wrow, brow)


def _peer_scores_body(h_ref, wq_ref, k_ref, o_ref):
    q = jnp.dot(h_ref[...], wq_ref[...], preferred_element_type=F32).astype(BF16)
    hb, _, _, half = k_ref.shape
    for hh in range(hb):
        for i in range(2):
            c0 = (hh * 2 + i) * half
            o_ref[hh, i] = lax.dot_general(k_ref[hh, i], q[:, c0:c0 + half], _NT, preferred_element_type=F32)


def _peer_scores(h2, wq, keys, tt):
    m, d = h2.shape
    heads, _, nk, half = keys.shape
    hb = _row_tile(heads, 4)
    return pl.pallas_call(
        _peer_scores_body,
        grid=(heads // hb, m // tt),
        in_specs=[pl.BlockSpec((tt, d), lambda h, t: (t, 0)),
                  pl.BlockSpec((d, hb * 2 * half), lambda h, t: (0, h)),
                  pl.BlockSpec((hb, 2, nk, half), lambda h, t: (h, 0, 0, 0))],
        out_specs=pl.BlockSpec((hb, 2, nk, tt), lambda h, t: (h, 0, 0, t)),
        out_shape=jax.ShapeDtypeStruct((heads, 2, nk, m), F32),
        compiler_params=_cparams("arbitrary", "arbitrary"),
        name="peer_scores",
    )(h2, wq, keys)


def _top_values(s, k):
    n = s.shape[0]
    ridx = lax.broadcasted_iota(jnp.int32, s.shape, 0)
    kidx = lax.broadcasted_iota(jnp.int32, (k, s.shape[1]), 0)
    out = jnp.zeros((k, s.shape[1]), F32)
    for r in range(k):
        m = jnp.max(s, axis=0, keepdims=True)
        first = jnp.min(jnp.where(s == m, ridx, n), axis=0, keepdims=True)
        s = jnp.where(ridx == first, -jnp.inf, s)
        out = jnp.where(kidx == r, m, out)
    return out


def _sorted_top_values(s, k):
    n, t = s.shape
    assert n == SUBLANES * k and k & (k - 1) == 0
    x = [s[r * SUBLANES:(r + 1) * SUBLANES, :] for r in range(k)]

    def merge_descending(c):
        stride = k // 2
        while stride >= 1:
            for i in range(k):
                l = i ^ stride
                if l > i:
                    c[i], c[l] = jnp.maximum(c[i], c[l]), jnp.minimum(c[i], c[l])
            stride //= 2
        return c

    size = 2
    while size <= k:
        stride = size // 2
        while stride >= 1:
            for i in range(k):
                l = i ^ stride
                if l > i:
                    hi, lo = jnp.maximum(x[i], x[l]), jnp.minimum(x[i], x[l])
                    x[i], x[l] = (hi, lo) if (i & size) == 0 else (lo, hi)
            stride //= 2
        size *= 2
    shift = SUBLANES // 2
    while shift >= 1:
        y = [pltpu.roll(a, shift, axis=0) for a in x]
        x = merge_descending([jnp.maximum(x[i], y[k - 1 - i]) for i in range(k)])
        shift //= 2
    row = lax.broadcasted_iota(jnp.int32, (SUBLANES, t), 0)
    halves = []
    for h0 in range(0, k, SUBLANES):
        acc = x[h0]
        for r in range(1, SUBLANES):
            acc = jnp.where(row == r, x[h0 + r], acc)
        halves.append(acc)
    return jnp.concatenate(halves, axis=0) if len(halves) > 1 else halves[0]


def _pair_candidates(v1, v2, k):
    isplit = 4
    lanes = v1.shape[1]

    def rows(nr):
        return lax.broadcasted_iota(jnp.int32, (nr, lanes), 0)

    pieces = []
    for i in range(min(isplit, k)):
        nj = k // (i + 1)
        nr = -(-nj // SUBLANES) * SUBLANES
        p = v1[i:i + 1, :] + v2[0:nr, :]
        pieces.append(p if nr == nj else jnp.where(rows(nr) < nj, p, -jnp.inf))
    for j in range(k // (isplit + 1)):
        ni = k // (j + 1)
        nr = -(-ni // SUBLANES) * SUBLANES
        r = rows(nr)
        pieces.append(jnp.where((r >= isplit) & (r < ni), v1[0:nr, :] + v2[j:j + 1, :], -jnp.inf))
    return jnp.concatenate(pieces, axis=0)


def _peer_stats_body(s_ref, o_ref, *, topk):
    heads, _, nk, _ = s_ref.shape
    sub_key_top = _sorted_top_values if nk == SUBLANES * topk else _top_values
    for h in range(heads):
        v1 = sub_key_top(s_ref[h, 0], topk)
        v2 = sub_key_top(s_ref[h, 1], topk)
        top = _top_values(_pair_candidates(v1, v2, topk), topk)
        mx = top[0:1, :]
        lse = mx + jnp.log(jnp.sum(jnp.exp(top - mx), axis=0, keepdims=True))
        o_ref[h, 0:1, :] = top[topk - 1:topk, :]
        o_ref[h, 1:2, :] = lse


def _peer_stats(scores, tt):
    heads, _, nk, m = scores.shape
    return pl.pallas_call(
        functools.partial(_peer_stats_body, topk=PEER_TOPK),
        grid=(m // tt,),
        in_specs=[pl.BlockSpec((heads, 2, nk, tt), lambda t: (0, 0, 0, t))],
        out_specs=pl.BlockSpec((heads, PEER_STATS, tt), lambda t: (0, 0, t)),
        out_shape=jax.ShapeDtypeStruct((heads, PEER_STATS, m), F32),
        compiler_params=_cparams("arbitrary"),
        name="peer_stats",
    )(scores)


def _peer_mix_body(h_ref, u_ref, v_ref, s_ref, st_ref, o_ref, *, nk):
    e = pl.program_id(1)

    @pl.when(e == 0)
    def _():
        o_ref[...] = jnp.zeros_like(o_ref)

    et = u_ref.shape[0]
    heads = s_ref.shape[0]
    act = _gelu(lax.dot_general(u_ref[...], h_ref[...], _NT, preferred_element_type=F32))
    parts = []
    for ii in range(et // nk):
        i = e * (et // nk) + ii
        wsum = jnp.zeros((nk, act.shape[1]), F32)
        for h in range(heads):
            c = s_ref[h, 0, pl.ds(i, 1), :] + s_ref[h, 1]
            wsum = wsum + jnp.where(c >= st_ref[h, 0:1, :], jnp.exp(c - st_ref[h, 1:2, :]), 0.0)
        parts.append((wsum * act[ii * nk:(ii + 1) * nk, :]).astype(BF16))
    wg = jnp.concatenate(parts, axis=0) if len(parts) > 1 else parts[0]
    o_ref[...] += lax.dot_general(wg, v_ref[...].astype(BF16), _TN, preferred_element_type=F32)


def _peer_mix(h2, u, v, scores, stats, tt, et):
    m, d = h2.shape
    ne = u.shape[0]
    heads, _, nk, _ = scores.shape
    return pl.pallas_call(
        functools.partial(_peer_mix_body, nk=nk),
        grid=(m // tt, ne // et),
        in_specs=[pl.BlockSpec((tt, d), lambda t, e: (t, 0)),
                  pl.BlockSpec((et, d), lambda t, e: (e, 0)),
                  pl.BlockSpec((et, d), lambda t, e: (e, 0)),
                  pl.BlockSpec((heads, 2, nk, tt), lambda t, e: (0, 0, 0, t)),
                  pl.BlockSpec((heads, PEER_STATS, tt), lambda t, e: (0, 0, t))],
        out_specs=pl.BlockSpec((tt, d), lambda t, e: (t, 0)),
        out_shape=jax.ShapeDtypeStruct((m, d), F32),
        compiler_params=_cparams("arbitrary", "arbitrary"),
        name="peer_mix",
    )(h2, u, v, scores, stats)


def _final_body(x_ref, p_ref, gate_ref, fg_ref, o_ref):
    xn = x_ref[0] + gate_ref[0] * p_ref[...]
    o_ref[0] = _mean_sq_norm(xn) * fg_ref[...]


def _final(x3, peer, row0, mod, k_gate, final_g):
    gg, r, d = x3.shape
    tm = _row_tile(r, 256)
    nb = r // tm
    r0 = row0 // tm
    return pl.pallas_call(
        _final_body,
        grid=(gg, nb),
        in_specs=[pl.BlockSpec((1, tm, d), lambda b, i: (b, i, 0)),
                  pl.BlockSpec((tm, d), lambda b, i: (r0 + b * nb + i, 0)),
                  _mod_spec(mod, k_gate, tm, d),
                  pl.BlockSpec((1, d), lambda b, i: (0, 0))],
        out_specs=pl.BlockSpec((1, tm, d), lambda b, i: (b, i, 0)),
        out_shape=jax.ShapeDtypeStruct((gg, r, d), F32),
        compiler_params=_cparams("arbitrary", "arbitrary"),
        name="final",
    )(x3, peer, mod, final_g)


def _mixer(x3, mod, w, dims, *, sample, ssm0=None, conv0=None):
    d, d_inner, groups, d_state, headdim, conv_dim = dims
    heads = d_inner // headdim
    hpg = heads // groups
    h = _norm_mod(x3, w["norm1_g"], mod, 0, 1)
    z = _matmul(h, w["w_z"], _ep_none, name="in_z")
    xbc = _matmul(h, w["w_xbc"], _ep_none, name="in_xbc")
    dt = _matmul(h, w["w_dt"], _ep_softplus_bias, extras=[(w["dt_bias"], "bias")], name="in_dt")
    guv = _matmul(h, w["w_uv"], _ep_gelu, name="in_uv")
    sg = _matmul(h, w["w_g"], _ep_sigmoid, name="in_gates")

    if not sample:
        b, ln, _ = x3.shape
        xbc_act = _conv_prompt(xbc, w["conv_w"], w["conv_b"])
        conv_new = xbc[:, ln - (w["conv_w"].shape[0] - 1):, :]
        dt_g = jnp.pad(dt.reshape(b, ln, groups, hpg), ((0, 0), (0, 0), (0, 0), (0, LANES - hpg)))
        dt_g = dt_g.reshape(b, ln, groups * LANES)
        yn, h_new = _ssd_prompt(xbc_act, dt_g, z, w["alog_g"], w["dsk_e"], w["ssd_norm_g"],
                                d_inner=d_inner, groups=groups, d_state=d_state, headdim=headdim)
        ssm_new = h_new.reshape(b, heads, headdim, d_state)
        yb = _gmlp_prompt(guv, w["gmlp_ln_g"], w["gmlp_ln_b"], w["gmlp_ws"], w["gmlp_bs_t"])
        v_rows = None
    else:
        b = conv0.shape[0]
        ln = x3.shape[1] // b
        cw = w["conv_w"].shape[0]

        def slabs(t):
            return t.reshape(ln, b, t.shape[-1])

        def flat(t):
            return t.reshape(1, ln * b, t.shape[-1])

        xp = jnp.concatenate([jnp.transpose(conv0, (1, 0, 2)), slabs(xbc)], axis=0)
        xbc_act = _conv_slab(xp, w["conv_w"], w["conv_b"])
        conv_new = jnp.transpose(xp[ln:ln + cw - 1], (1, 0, 2))

        def to_rows(t):
            return jnp.pad(jnp.transpose(t, (1, 0, 2)), ((0, 0), (0, SUBLANES - ln), (0, 0)))

        dte_b = jnp.repeat(to_rows(slabs(dt)), headdim, axis=2)
        yn_b, h_new = _ssd_sample(to_rows(xbc_act), dte_b, to_rows(slabs(z)), w["a_e"], w["dsk_e"],
                                  w["ssd_norm_g"], ssm0.reshape(b, d_inner, d_state), groups=groups, valid=ln)
        ssm_new = h_new.reshape(b, heads, headdim, d_state)
        yn = flat(jnp.transpose(yn_b[:, :ln], (1, 0, 2)).astype(BF16))
        wrow, brow = _slab_gate_rows(w, ln)
        yb, v_tm = _gmlp_slab(slabs(guv), w["gmlp_ln_g"], w["gmlp_ln_b"], wrow, brow)
        yb = flat(yb)
        v_rows = jnp.transpose(v_tm, (1, 0, 2))

    y_a = _matmul(yn, w["w_oa"], _ep_none, name="out_a")
    merged = _matmul(yb, w["w_ob"], _ep_merge, extras=[(y_a, "tile"), (sg, "tile", 0), (sg, "tile", 1)],
                     out_dtype=BF16, name="out_b_merge")
    x1 = _matmul(merged, w["w_out"], _ep_residual, extras=[(x3, "tile"), (mod, "mod", 2)], name="out_proj")
    h2 = _norm_mod(x1, w["norm2_g"], mod, 3, 4)
    return x1, h2, ssm_new, conv_new, v_rows


def _slab_gate_rows(w, ln):
    ws, bs = w["gmlp_ws"], w["gmlp_bs"]
    gw = w["gmlp_ln_g"].shape[1] // ws.shape[0]
    wt = jnp.where(jnp.tril(jnp.ones((ln, ln), bool)), ws[:, :ln, :ln], 0.0)
    wrow = jnp.repeat(jnp.transpose(wt, (1, 2, 0)).reshape(ln * ln, -1), gw, axis=1)
    brow = jnp.repeat(jnp.transpose(bs[:, :ln]), gw, axis=1)
    return wrow, brow


def kernel(x_prompt, x_sample, state_ssm, state_conv, c_prompt, c_sample, w_ada, b_ada, norm1_g, w_in, conv_w, conv_b, dt_bias, a_log, d_skip, ssd_norm_g, w_oa, gmlp_ln_g, gmlp_ln_b, gmlp_ws, gmlp_bs, w_ob, w_out, norm2_g, peer_wq, peer_keys, peer_u, peer_v, final_g):
    depth = w_ada.shape[0]
    bp, lp, d = x_prompt.shape
    bs_, ls, _ = x_sample.shape
    heads = a_log.shape[1]
    headdim, d_state = state_ssm.shape[3], state_ssm.shape[4]
    d_inner = heads * headdim
    conv_dim = conv_w.shape[2]
    groups = (conv_dim - d_inner) // (2 * d_state)
    hpg = heads // groups
    d_gmlp = gmlp_ln_g.shape[1]
    dims = (d, d_inner, groups, d_state, headdim, conv_dim)
    off_xbc = d_inner
    off_dt = off_xbc + conv_dim
    off_u = off_dt + heads
    off_ga = off_u + 2 * d_gmlp

    xp = x_prompt
    xs = jnp.transpose(x_sample, (1, 0, 2)).reshape(1, ls * bs_, d)
    n_c = bs_ + bp
    c_all = jnp.pad(jnp.concatenate([c_sample, c_prompt], axis=0), ((0, (-n_c) % SUBLANES), (0, 0)))
    ssm_p, conv_p, ssm_s, conv_s, v_s = [], [], [], [], []
    for l in range(depth):
        mod = _ada(c_all, w_ada[l], b_ada[l][None, :])
        mod_s = jnp.tile(mod[:bs_], (ls, 1))[None]
        mod_p = mod[bs_:n_c][:, None, :]
        wl = w_in[l].astype(BF16)

        def family(start, n):
            return (wl, (start, n)) if start % LANES == 0 and n % LANES == 0 else wl[:, start:start + n]

        a = -jnp.exp(a_log[l])
        w = dict(
            norm1_g=norm1_g[l][None, :], norm2_g=norm2_g[l][None, :],
            w_z=family(0, d_inner), w_xbc=family(off_xbc, conv_dim), w_dt=family(off_dt, heads),
            w_uv=wl[:, off_u:off_ga], w_g=wl[:, off_ga:], dt_bias=dt_bias[l][None, :],
            conv_w=conv_w[l], conv_b=conv_b[l][None, :],
            alog_g=jnp.pad(a_log[l].reshape(groups, hpg), ((0, 0), (0, LANES - hpg))).reshape(1, groups * LANES),
            a_e=jnp.repeat(a, headdim)[None, :], dsk_e=jnp.repeat(d_skip[l], headdim)[None, :],
            ssd_norm_g=ssd_norm_g[l][None, :],
            w_oa=w_oa[l].astype(BF16), w_ob=w_ob[l].astype(BF16), w_out=w_out[l].astype(BF16),
            gmlp_ln_g=gmlp_ln_g[l][None, :], gmlp_ln_b=gmlp_ln_b[l][None, :],
            gmlp_ws=gmlp_ws[l], gmlp_bs=gmlp_bs[l], gmlp_bs_t=jnp.transpose(gmlp_bs[l]),
        )
        x1p, h2p, hp, cp, _ = _mixer(xp, mod_p, w, dims, sample=False)
        x1s, h2s, hs, cs, vs = _mixer(xs, mod_s, w, dims, sample=True, ssm0=state_ssm[l], conv0=state_conv[l])

        h2 = jnp.concatenate([h2p.reshape(bp * lp, d), h2s.reshape(ls * bs_, d)], axis=0)
        m = h2.shape[0]
        tt = _row_tile(m, 512)
        ne = peer_u.shape[1]
        et = _row_tile(ne, 256)
        scores = _peer_scores(h2, peer_wq[l].astype(BF16), peer_keys[l].astype(BF16), tt)
        stats = _peer_stats(scores, tt)
        peer_t = _peer_mix(h2, peer_u[l].astype(BF16), peer_v[l], scores, stats, tt, et)
        last = l == depth - 1
        fg = final_g[None, :] if last else jnp.ones((1, d), F32)
        assert last, "only DEPTH == 1 is supported"
        xp = _final(x1p, peer_t, 0, mod_p, 5, fg)
        xs = _final(x1s, peer_t, bp * lp, mod_s, 5, fg)
        ssm_p.append(hp)
        conv_p.append(cp)
        ssm_s.append(hs)
        conv_s.append(cs)
        v_s.append(vs)
    y_prompt = xp
    y_sample = jnp.transpose(xs.reshape(ls, bs_, d), (1, 0, 2))
    return (y_prompt, y_sample, jnp.stack(ssm_p), jnp.stack(conv_p), jnp.stack(ssm_s), jnp.stack(conv_s),
            jnp.stack(v_s))
```

```python
import functools
import math

import jax
import jax.numpy as jnp
from jax import lax
from jax.experimental import pallas as pl
from jax.experimental.pallas import tpu as pltpu

F32 = jnp.float32
BF16 = jnp.bfloat16
EPS = 1e-6
N_MOD = 6
PEER_TOPK = 16
PEER_STATS = 2
SSD_CHUNK = 128
SSD_GROUPS_PER_STEP = 2
GMLP_CHUNK = 128
SUBLANES = 8
LANES = 128
VMEM_LIMIT = 56 * 1024 * 1024
MATMUL_VMEM_BUDGET = 48 * 1024 * 1024
FEATURE_TILE = 1024
_SQRT_HALF = 0.7071067811865476
_HIGHEST = lax.Precision.HIGHEST
_NT = (((1,), (1,)), ((), ()))
_TN = (((0,), (0,)), ((), ()))


def _cparams(*sem):
    return pltpu.CompilerParams(dimension_semantics=sem, vmem_limit_bytes=VMEM_LIMIT)


def _silu(x):
    return x * jax.nn.sigmoid(x)


def _gelu(x):
    return 0.5 * x * (1.0 + lax.erf(x * _SQRT_HALF))


def _softplus(x):
    return jnp.maximum(x, 0.0) + jnp.log1p(jnp.exp(-jnp.abs(x)))


def _mean_sq_norm(x):
    return x * lax.rsqrt(jnp.mean(x * x, axis=-1, keepdims=True) + EPS)


def _ada_body(c_ref, w_ref, b_ref, o_ref):
    a = _silu(c_ref[...]).astype(BF16)
    o_ref[...] = jnp.dot(a, w_ref[...].astype(BF16), preferred_element_type=F32) + b_ref[...]


def _ada(c_all, w_ada, b_ada):
    m, d = c_all.shape
    n = w_ada.shape[1]
    tn = min(512, n)
    return pl.pallas_call(
        _ada_body,
        grid=(n // tn,),
        in_specs=[pl.BlockSpec((m, d), lambda j: (0, 0)),
                  pl.BlockSpec((d, tn), lambda j: (0, j)),
                  pl.BlockSpec((1, tn), lambda j: (0, j))],
        out_specs=pl.BlockSpec((m, tn), lambda j: (0, j)),
        out_shape=jax.ShapeDtypeStruct((m, n), F32),
        compiler_params=_cparams("arbitrary"),
        name="ada",
    )(c_all, w_ada, b_ada)


def _mod_block(mod, k, tm, tn):
    nblk = mod.shape[2] // N_MOD // tn
    if mod.shape[1] == 1:
        return (1, 1, tn), lambda j, g, i: (g, 0, k * nblk + j)
    return (1, tm, tn), lambda j, g, i: (0, i, k * nblk + j)


def _mod_spec(mod, k, tm, d):
    shape, index = _mod_block(mod, k, tm, d)
    return pl.BlockSpec(shape, lambda g, i: index(0, g, i))


def _row_tile(r, target):
    t = min(r, target)
    while r % t:
        t //= 2
    return t


def _norm_mod_body(x_ref, g_ref, sc_ref, sh_ref, o_ref):
    y = _mean_sq_norm(x_ref[0]) * g_ref[...]
    o_ref[0] = (y * (1.0 + sc_ref[0]) + sh_ref[0]).astype(o_ref.dtype)


def _norm_mod(x3, g, mod, k_shift, k_scale):
    gg, r, d = x3.shape
    tm = _row_tile(r, 512)
    return pl.pallas_call(
        _norm_mod_body,
        grid=(gg, r // tm),
        in_specs=[pl.BlockSpec((1, tm, d), lambda b, i: (b, i, 0)),
                  pl.BlockSpec((1, d), lambda b, i: (0, 0)),
                  _mod_spec(mod, k_scale, tm, d),
                  _mod_spec(mod, k_shift, tm, d)],
        out_specs=pl.BlockSpec((1, tm, d), lambda b, i: (b, i, 0)),
        out_shape=jax.ShapeDtypeStruct((gg, r, d), BF16),
        compiler_params=_cparams("arbitrary", "arbitrary"),
        name="norm_mod",
    )(x3, g, mod, mod)


def _mm_body(*refs, epilogue, kinds):
    a_ref, w_ref = refs[0], refs[1]
    o_ref = refs[-1]
    acc = jnp.dot(a_ref[0], w_ref[...], preferred_element_type=F32)
    ex = [r[...] if kd == "bias" else r[0] for r, kd in zip(refs[2:-1], kinds)]
    o_ref[0] = epilogue(acc, *ex).astype(o_ref.dtype)


def _matmul(a3, w, epilogue, extras=(), out_dtype=F32, name="mm"):
    gg, r, k = a3.shape
    w, (c0, n) = w if isinstance(w, tuple) else (w, (0, w.shape[1]))
    tm = _row_tile(r, 1024)
    tn = _row_tile(math.gcd(n, c0), FEATURE_TILE)
    tile_bytes = sum(e[0].dtype.itemsize for e in extras if e[1] == "tile") + jnp.dtype(out_dtype).itemsize

    def vmem_bytes(tm_, tn_):
        return 2 * (tm_ * k * 2 + k * tn_ * 2 + tm_ * tn_ * tile_bytes) + 2 * tm_ * tn_ * 4

    while vmem_bytes(tm, tn) > MATMUL_VMEM_BUDGET and (tm > 256 or tn > 256):
        if (tm >= tn and tm > 256) or tn <= 256:
            tm //= 2
        else:
            tn //= 2
    assert c0 % tn == 0
    in_specs = [pl.BlockSpec((1, tm, k), lambda j, g, i: (g, i, 0)),
                pl.BlockSpec((k, tn), lambda j, g, i: (0, c0 // tn + j))]
    args = [a3, w]
    kinds = []
    for e in extras:
        arr, kind = e[0], e[1]
        kinds.append(kind)
        args.append(arr)
        if kind == "tile":
            off = e[2] if len(e) > 2 else 0
            nb = n // tn
            in_specs.append(pl.BlockSpec((1, tm, tn), lambda j, g, i, off=off, nb=nb: (g, i, off * nb + j)))
        elif kind == "bias":
            in_specs.append(pl.BlockSpec((1, tn), lambda j, g, i: (0, j)))
        else:
            in_specs.append(pl.BlockSpec(*_mod_block(arr, e[2], tm, tn)))
    return pl.pallas_call(
        functools.partial(_mm_body, epilogue=epilogue, kinds=tuple(kinds)),
        grid=(n // tn, gg, r // tm),
        in_specs=in_specs,
        out_specs=pl.BlockSpec((1, tm, tn), lambda j, g, i: (g, i, j)),
        out_shape=jax.ShapeDtypeStruct((gg, r, n), out_dtype),
        compiler_params=_cparams("arbitrary", "arbitrary", "arbitrary"),
        name=name,
    )(*args)


def _ep_none(acc):
    return acc


def _ep_gelu(acc):
    return _gelu(acc)


def _ep_sigmoid(acc):
    return jax.nn.sigmoid(acc)


def _ep_softplus_bias(acc, bias):
    return _softplus(acc + bias)


def _ep_merge(acc, y_a, sg_a, sg_b):
    return sg_a * y_a + sg_b * acc


def _ep_residual(acc, x, gate):
    return x + gate * acc


def _conv_prompt_body(x_ref, w_ref, b_ref, o_ref, pad_ref, *, width):
    ln = x_ref.shape[1]
    pad_ref[0:SUBLANES, :] = jnp.zeros((SUBLANES, pad_ref.shape[1]), F32)
    pad_ref[SUBLANES:SUBLANES + ln, :] = x_ref[0]
    acc = b_ref[...] + x_ref[0] * w_ref[width - 1:width, :]
    for k in range(width - 1):
        s = SUBLANES - (width - 1) + k
        acc = acc + pad_ref[s:s + ln, :] * w_ref[k:k + 1, :]
    o_ref[0] = _silu(acc)


def _conv_prompt(x3, conv_w, conv_b):
    b, ln, c = x3.shape
    width = conv_w.shape[0]
    ct = _row_tile(c, 512)
    return pl.pallas_call(
        functools.partial(_conv_prompt_body, width=width),
        grid=(b, c // ct),
        in_specs=[pl.BlockSpec((1, ln, ct), lambda i, j: (i, 0, j)),
                  pl.BlockSpec((width, ct), lambda i, j: (0, j)),
                  pl.BlockSpec((1, ct), lambda i, j: (0, j))],
        out_specs=pl.BlockSpec((1, ln, ct), lambda i, j: (i, 0, j)),
        out_shape=jax.ShapeDtypeStruct((b, ln, c), F32),
        scratch_shapes=[pltpu.VMEM((ln + SUBLANES, ct), F32)],
        compiler_params=_cparams("arbitrary", "arbitrary"),
        name="conv_prompt",
    )(x3, conv_w, conv_b)


def _conv_slab_body(xp_ref, w_ref, b_ref, o_ref, *, width):
    for t in range(o_ref.shape[0]):
        acc = b_ref[...] + xp_ref[t] * w_ref[0:1, :]
        for k in range(1, width):
            acc = acc + xp_ref[t + k] * w_ref[k:k + 1, :]
        o_ref[t] = _silu(acc)


def _conv_slab(xp, conv_w, conv_b):
    lp, b, c = xp.shape
    width = conv_w.shape[0]
    ln = lp - width + 1
    ct = _row_tile(c, 1024)
    return pl.pallas_call(
        functools.partial(_conv_slab_body, width=width),
        grid=(c // ct,),
        in_specs=[pl.BlockSpec((lp, b, ct), lambda j: (0, 0, j)),
                  pl.BlockSpec((width, ct), lambda j: (0, j)),
                  pl.BlockSpec((1, ct), lambda j: (0, j))],
        out_specs=pl.BlockSpec((ln, b, ct), lambda j: (0, 0, j)),
        out_shape=jax.ShapeDtypeStruct((ln, b, c), F32),
        compiler_params=_cparams("arbitrary"),
        name="conv_sample",
    )(xp, conv_w, conv_b)


def _ssd_prompt_body(x_ref, b_ref, c_ref, dt_ref, z_ref, alog_ref, dsk_ref, ng_ref, y_ref, hl_ref,
                     h_scr, y_scr, *, headdim, gps):
    ci = pl.program_id(2)

    @pl.when(ci == 0)
    def _():
        h_scr[...] = jnp.zeros_like(h_scr)

    cl = x_ref.shape[1]
    gw = x_ref.shape[2] // gps
    nst = b_ref.shape[2] // gps
    row = lax.broadcasted_iota(jnp.int32, (cl, cl), 0)
    col = lax.broadcasted_iota(jnp.int32, (cl, cl), 1)
    tril = row >= col
    acum_all = jnp.dot(tril.astype(F32), dt_ref[0] * -jnp.exp(alog_ref[...]), precision=_HIGHEST,
                       preferred_element_type=F32)
    lane_head = lax.broadcasted_iota(jnp.int32, (cl, LANES), 1) // headdim
    heads_per_blk = LANES // headdim
    for gi in range(gps):
        g0 = gi * gw
        x = x_ref[0][:, g0:g0 + gw]
        dt = dt_ref[0][:, gi * LANES:(gi + 1) * LANES]
        acum = acum_all[:, gi * LANES:(gi + 1) * LANES]
        acum_t = acum.T
        bb = b_ref[0][:, gi * nst:(gi + 1) * nst].astype(BF16)
        cc = c_ref[0][:, gi * nst:(gi + 1) * nst].astype(BF16)
        cb = lax.dot_general(cc, bb, _NT, preferred_element_type=F32)
        h_t = h_scr[:, g0:g0 + gw]
        y_off = jnp.dot(cc, h_t.astype(BF16), preferred_element_type=F32)
        for j in range(gw // LANES):
            sl = slice(j * LANES, (j + 1) * LANES)
            gsl = slice(g0 + j * LANES, g0 + (j + 1) * LANES)
            h0 = j * heads_per_blk
            dte = jnp.broadcast_to(dt[:, h0:h0 + 1], (cl, LANES))
            ace = jnp.broadcast_to(acum[:, h0:h0 + 1], (cl, LANES))
            for q in range(1, heads_per_blk):
                dte = jnp.where(lane_head == q, dt[:, h0 + q:h0 + q + 1], dte)
                ace = jnp.where(lane_head == q, acum[:, h0 + q:h0 + q + 1], ace)
            xj = x[:, sl]
            xdt = xj * dte
            last = ace[cl - 1:cl, :]
            yd = jnp.zeros((cl, LANES), F32)
            for q in range(heads_per_blk):
                hh = h0 + q
                seg = acum[:, hh:hh + 1] - acum_t[hh:hh + 1, :]
                m = (cb * jnp.exp(jnp.where(tril, seg, -jnp.inf))).astype(BF16)
                xm = jnp.where(lane_head == q, xdt, 0.0).astype(BF16)
                yd = yd + jnp.dot(m, xm, preferred_element_type=F32)
            y = yd + y_off[:, sl] * jnp.exp(ace) + dsk_ref[:, gsl] * xj
            y_scr[:, gsl] = y * _silu(z_ref[0][:, gsl])
            st_t = lax.dot_general(bb, (xdt * jnp.exp(last - ace)).astype(BF16), _TN,
                                   preferred_element_type=F32)
            h_scr[:, gsl] = h_t[:, sl] * jnp.exp(last) + st_t
        y_ref[0, :, g0:g0 + gw] = (_mean_sq_norm(y_scr[:, g0:g0 + gw]) * ng_ref[:, g0:g0 + gw]).astype(y_ref.dtype)

    @pl.when(ci == pl.num_programs(2) - 1)
    def _():
        hl_ref[0] = h_scr[...].T


def _ssd_prompt(xbc_act, dt_g, z, alog_g, dsk_e, norm_g, *, d_inner, groups, d_state, headdim):
    b, ln, _ = xbc_act.shape
    gw = d_inner // groups
    cl = min(SSD_CHUNK, ln)
    nc = ln // cl
    gps = SSD_GROUPS_PER_STEP if groups % SSD_GROUPS_PER_STEP == 0 else 1
    gw *= gps
    sw = gps * d_state
    xb = d_inner // sw
    assert d_inner % sw == 0
    return pl.pallas_call(
        functools.partial(_ssd_prompt_body, headdim=headdim, gps=gps),
        grid=(b, groups // gps, nc),
        in_specs=[pl.BlockSpec((1, cl, gw), lambda i, g, c: (i, c, g)),
                  pl.BlockSpec((1, cl, sw), lambda i, g, c: (i, c, xb + g)),
                  pl.BlockSpec((1, cl, sw), lambda i, g, c: (i, c, xb + groups // gps + g)),
                  pl.BlockSpec((1, cl, gps * LANES), lambda i, g, c: (i, c, g)),
                  pl.BlockSpec((1, cl, gw), lambda i, g, c: (i, c, g)),
                  pl.BlockSpec((1, gps * LANES), lambda i, g, c: (0, g)),
                  pl.BlockSpec((1, gw), lambda i, g, c: (0, g)),
                  pl.BlockSpec((1, gw), lambda i, g, c: (0, g))],
        out_specs=[pl.BlockSpec((1, cl, gw), lambda i, g, c: (i, c, g)),
                   pl.BlockSpec((1, gw, d_state), lambda i, g, c: (i, g, 0))],
        out_shape=[jax.ShapeDtypeStruct((b, ln, d_inner), BF16),
                   jax.ShapeDtypeStruct((b, d_inner, d_state), F32)],
        scratch_shapes=[pltpu.VMEM((d_state, gw), F32), pltpu.VMEM((cl, gw), F32)],
        compiler_params=_cparams("arbitrary", "arbitrary", "arbitrary"),
        name="ssd_prompt",
    )(xbc_act, xbc_act, xbc_act, dt_g, z, alog_g, dsk_e, norm_g)


def _ssd_sample_body(x_ref, dte_ref, b_ref, c_ref, z_ref, ae_ref, dsk_ref, ng_ref, h0_ref, y_ref, hn_ref,
                     *, groups, valid):
    x = x_ref[0]
    rows, di = x.shape
    gw = di // groups
    n = h0_ref.shape[2]
    rix = lax.broadcasted_iota(jnp.int32, (rows, di), 0)
    dte = jnp.where(rix < valid, dte_ref[0], 0.0)
    ac = dte * ae_ref[...]
    for sh in (1, 2, 4):
        ac = ac + jnp.where(rix >= sh, pltpu.roll(ac, sh, axis=0), 0.0)
    xdt = x * dte
    last = ac[rows - 1:rows, :]
    cdec = jnp.exp(last)
    hi = cdec.astype(BF16)
    r1 = cdec - hi.astype(F32)
    mid = r1.astype(BF16)
    lo = (r1 - mid.astype(F32)).astype(BF16)
    lhs = (xdt * jnp.exp(last - ac)).astype(BF16)
    lhs = jnp.where(rix == valid, hi, jnp.where(rix == valid + 1, mid, jnp.where(rix == valid + 2, lo, lhs)))
    rn = lax.broadcasted_iota(jnp.int32, (rows, n), 0)
    ones_rows = jnp.where((rn >= valid) & (rn < valid + 3), 1.0, 0.0).astype(BF16)
    rg = lax.broadcasted_iota(jnp.int32, (rows, gw), 0)
    eac = jnp.exp(ac)
    for g in range(groups):
        sl = slice(g * gw, (g + 1) * gw)
        bg = b_ref[0][:, g * n:(g + 1) * n]
        cg = c_ref[0][:, g * n:(g + 1) * n]
        h0 = h0_ref[0, sl, :]
        y_off = lax.dot_general(cg.astype(BF16), h0.astype(BF16), _NT, preferred_element_type=F32)
        rhs = jnp.concatenate([jnp.where(rn < valid, bg, 0.0).astype(BF16), ones_rows], axis=1)
        res = lax.dot_general(lhs[:, sl], rhs, _TN, preferred_element_type=F32)
        hn_ref[0, sl, :] = h0 * res[:, n:] + res[:, :n]
        acg = ac[:, sl]
        xdtg = xdt[:, sl]
        yd = jnp.zeros((rows, gw), F32)
        for s in range(valid):
            cbs = jnp.sum(cg * bg[s:s + 1, :], axis=-1, keepdims=True)
            w = jnp.where(rg >= s, jnp.exp(acg - acg[s:s + 1, :]), 0.0)
            yd = yd + cbs * w * xdtg[s:s + 1, :]
        xg = x[:, sl]
        y = (yd + y_off * eac[:, sl] + dsk_ref[:, sl] * xg) * _silu(z_ref[0][:, sl])
        y_ref[0, :, sl] = (_mean_sq_norm(y) * ng_ref[:, sl]).astype(y_ref.dtype)


def _ssd_sample(xbc_b, dte_b, z_b, a_e, dsk_e, norm_g, h0, *, groups, valid):
    b, rows, _ = xbc_b.shape
    _, di, n = h0.shape
    gn = groups * n
    assert valid + 3 <= rows and di % gn == 0
    seq = lambda i: (i, 0, 0)
    par = lambda i: (0, 0)
    return pl.pallas_call(
        functools.partial(_ssd_sample_body, groups=groups, valid=valid),
        grid=(b,),
        in_specs=[pl.BlockSpec((1, rows, di), seq),
                  pl.BlockSpec((1, rows, di), seq),
                  pl.BlockSpec((1, rows, gn), lambda i: (i, 0, di // gn)),
                  pl.BlockSpec((1, rows, gn), lambda i: (i, 0, di // gn + 1)),
                  pl.BlockSpec((1, rows, di), seq),
                  pl.BlockSpec((1, di), par),
                  pl.BlockSpec((1, di), par),
                  pl.BlockSpec((1, di), par),
                  pl.BlockSpec((1, di, n), seq)],
        out_specs=[pl.BlockSpec((1, rows, di), seq),
                   pl.BlockSpec((1, di, n), seq)],
        out_shape=[jax.ShapeDtypeStruct((b, rows, di), F32),
                   jax.ShapeDtypeStruct((b, di, n), F32)],
        compiler_params=_cparams("arbitrary"),
        name="ssd_sample",
    )(xbc_b, dte_b, xbc_b, xbc_b, z_b, a_e, dsk_e, norm_g, h0)


def _layernorm(v, g, b):
    mu = jnp.mean(v, axis=-1, keepdims=True)
    d = v - mu
    var = jnp.mean(d * d, axis=-1, keepdims=True)
    return d * lax.rsqrt(var + EPS) * g + b


def _gmlp_prompt_body(gv_ref, gu_ref, lg_ref, lb_ref, ws_ref, bst_ref, o_ref):
    vn = _layernorm(gv_ref[0], lg_ref[...], lb_ref[...])
    cl, d = vn.shape
    groups = ws_ref.shape[0]
    gw = d // groups
    row = lax.broadcasted_iota(jnp.int32, (cl, cl), 0)
    col = lax.broadcasted_iota(jnp.int32, (cl, cl), 1)
    tril = row >= col
    for g in range(groups):
        sl = slice(g * gw, (g + 1) * gw)
        w = jnp.where(tril, ws_ref[g], 0.0).astype(BF16)
        s = jnp.dot(w, vn[:, sl].astype(BF16), preferred_element_type=F32) + bst_ref[:, g:g + 1]
        o_ref[0, :, sl] = (gu_ref[0][:, sl] * s).astype(o_ref.dtype)


def _gmlp_prompt(guv, ln_g, ln_b, ws, bs_t):
    b, ln, d2 = guv.shape
    d = d2 // 2
    cl = min(GMLP_CHUNK, ln)
    groups = ws.shape[0]
    return pl.pallas_call(
        _gmlp_prompt_body,
        grid=(b, ln // cl),
        in_specs=[pl.BlockSpec((1, cl, d), lambda i, c: (i, c, 1)),
                  pl.BlockSpec((1, cl, d), lambda i, c: (i, c, 0)),
                  pl.BlockSpec((1, d), lambda i, c: (0, 0)),
                  pl.BlockSpec((1, d), lambda i, c: (0, 0)),
                  pl.BlockSpec((groups, cl, cl), lambda i, c: (0, 0, 0)),
                  pl.BlockSpec((cl, groups), lambda i, c: (0, 0))],
        out_specs=pl.BlockSpec((1, cl, d), lambda i, c: (i, c, 0)),
        out_shape=jax.ShapeDtypeStruct((b, ln, d), BF16),
        compiler_params=_cparams("arbitrary", "arbitrary"),
        name="gmlp_prompt",
    )(guv, guv, ln_g, ln_b, ws, bs_t)


def _gmlp_slab_body(gv_ref, gu_ref, lg_ref, lb_ref, wrow_ref, brow_ref, o_ref, vn_ref):
    ln = gv_ref.shape[0]
    vn = [_layernorm(gv_ref[t], lg_ref[...], lb_ref[...]) for t in range(ln)]
    for t in range(ln):
        vn_ref[t] = vn[t]
        s = brow_ref[t:t + 1, :] + vn[0] * wrow_ref[t * ln:t * ln + 1, :]
        for k in range(1, t + 1):
            s = s + vn[k] * wrow_ref[t * ln + k:t * ln + k + 1, :]
        o_ref[t] = (gu_ref[t] * s).astype(o_ref.dtype)


def _gmlp_slab(guv, ln_g, ln_b, wrow, brow):
    ln, b, d2 = guv.shape
    d = d2 // 2
    bt = _row_tile(b, 32)
    return pl.pallas_call(
        _gmlp_slab_body,
        grid=(b // bt,),
        in_specs=[pl.BlockSpec((ln, bt, d), lambda i: (0, i, 1)),
                  pl.BlockSpec((ln, bt, d), lambda i: (0, i, 0)),
                  pl.BlockSpec((1, d), lambda i: (0, 0)),
                  pl.BlockSpec((1, d), lambda i: (0, 0)),
                  pl.BlockSpec((ln * ln, d), lambda i: (0, 0)),
                  pl.BlockSpec((ln, d), lambda i: (0, 0))],
        out_specs=[pl.BlockSpec((ln, bt, d), lambda i: (0, i, 0)),
                   pl.BlockSpec((ln, bt, d), lambda i: (0, i, 0))],
        out_shape=[jax.ShapeDtypeStruct((ln, b, d), BF16),
                   jax.ShapeDtypeStruct((ln, b, d), F32)],
        compiler_params=_cparams("arbitrary"),
        name="gmlp_sample",
    )(guv, guv, ln_g, ln_b, wrow, brow)


def _peer_scores_body(h_ref, wq_ref, k_ref, o_ref):
    q = jnp.dot(h_ref[...], wq_ref[...], preferred_element_type=F32).astype(BF16)
    hb, _, _, half = k_ref.shape
    for hh in range(hb):
        for i in range(2):
            c0 = (hh * 2 + i) * half
            o_ref[hh, i] = lax.dot_general(k_ref[hh, i], q[:, c0:c0 + half], _NT, preferred_element_type=F32)


def _peer_scores(h2, wq, keys, tt):
    m, d = h2.shape
    heads, _, nk, half = keys.shape
    hb = _row_tile(heads, 4)
    return pl.pallas_call(
        _peer_scores_body,
        grid=(heads // hb, m // tt),
        in_specs=[pl.BlockSpec((tt, d), lambda h, t: (t, 0)),
                  pl.BlockSpec((d, hb * 2 * half), lambda h, t: (0, h)),
                  pl.BlockSpec((hb, 2, nk, half), lambda h, t: (h, 0, 0, 0))],
        out_specs=pl.BlockSpec((hb, 2, nk, tt), lambda h, t: (h, 0, 0, t)),
        out_shape=jax.ShapeDtypeStruct((heads, 2, nk, m), F32),
        compiler_params=_cparams("arbitrary", "arbitrary"),
        name="peer_scores",
    )(h2, wq, keys)


def _top_values(s, k):
    n = s.shape[0]
    ridx = lax.broadcasted_iota(jnp.int32, s.shape, 0)
    kidx = lax.broadcasted_iota(jnp.int32, (k, s.shape[1]), 0)
    out = jnp.zeros((k, s.shape[1]), F32)
    for r in range(k):
        m = jnp.max(s, axis=0, keepdims=True)
        first = jnp.min(jnp.where(s == m, ridx, n), axis=0, keepdims=True)
        s = jnp.where(ridx == first, -jnp.inf, s)
        out = jnp.where(kidx == r, m, out)
    return out


def _sorted_top_values(s, k):
    n, t = s.shape
    assert n == SUBLANES * k and k & (k - 1) == 0
    x = [s[r * SUBLANES:(r + 1) * SUBLANES, :] for r in range(k)]

    def merge_descending(c):
        stride = k // 2
        while stride >= 1:
            for i in range(k):
                l = i ^ stride
                if l > i:
                    c[i], c[l] = jnp.maximum(c[i], c[l]), jnp.minimum(c[i], c[l])
            stride //= 2
        return c

    size = 2
    while size <= k:
        stride = size // 2
        while stride >= 1:
            for i in range(k):
                l = i ^ stride
                if l > i:
                    hi, lo = jnp.maximum(x[i], x[l]), jnp.minimum(x[i], x[l])
                    x[i], x[l] = (hi, lo) if (i & size) == 0 else (lo, hi)
            stride //= 2
        size *= 2
    shift = SUBLANES // 2
    while shift >= 1:
        y = [pltpu.roll(a, shift, axis=0) for a in x]
        x = merge_descending([jnp.maximum(x[i], y[k - 1 - i]) for i in range(k)])
        shift //= 2
    row = lax.broadcasted_iota(jnp.int32, (SUBLANES, t), 0)
    halves = []
    for h0 in range(0, k, SUBLANES):
        acc = x[h0]
        for r in range(1, SUBLANES):
            acc = jnp.where(row == r, x[h0 + r], acc)
        halves.append(acc)
    return jnp.concatenate(halves, axis=0) if len(halves) > 1 else halves[0]


def _pair_candidates(v1, v2, k):
    isplit = 4
    lanes = v1.shape[1]

    def rows(nr):
        return lax.broadcasted_iota(jnp.int32, (nr, lanes), 0)

    pieces = []
    for i in range(min(isplit, k)):
        nj = k // (i + 1)
        nr = -(-nj // SUBLANES) * SUBLANES
        p = v1[i:i + 1, :] + v2[0:nr, :]
        pieces.append(p if nr == nj else jnp.where(rows(nr) < nj, p, -jnp.inf))
    for j in range(k // (isplit + 1)):
        ni = k // (j + 1)
        nr = -(-ni // SUBLANES) * SUBLANES
        r = rows(nr)
        pieces.append(jnp.where((r >= isplit) & (r < ni), v1[0:nr, :] + v2[j:j + 1, :], -jnp.inf))
    return jnp.concatenate(pieces, axis=0)


def _peer_stats_body(s_ref, o_ref, *, topk):
    heads, _, nk, _ = s_ref.shape
    sub_key_top = _sorted_top_values if nk == SUBLANES * topk else _top_values
    for h in range(heads):
        v1 = sub_key_top(s_ref[h, 0], topk)
        v2 = sub_key_top(s_ref[h, 1], topk)
        top = _top_values(_pair_candidates(v1, v2, topk), topk)
        mx = top[0:1, :]
        lse = mx + jnp.log(jnp.sum(jnp.exp(top - mx), axis=0, keepdims=True))
        o_ref[h, 0:1, :] = top[topk - 1:topk, :]
        o_ref[h, 1:2, :] = lse


def _peer_stats(scores, tt):
    heads, _, nk, m = scores.shape
    return pl.pallas_call(
        functools.partial(_peer_stats_body, topk=PEER_TOPK),
        grid=(m // tt,),
        in_specs=[pl.BlockSpec((heads, 2, nk, tt), lambda t: (0, 0, 0, t))],
        out_specs=pl.BlockSpec((heads, PEER_STATS, tt), lambda t: (0, 0, t)),
        out_shape=jax.ShapeDtypeStruct((heads, PEER_STATS, m), F32),
        compiler_params=_cparams("arbitrary"),
        name="peer_stats",
    )(scores)


def _peer_mix_body(h_ref, u_ref, v_ref, s_ref, st_ref, o_ref, *, nk):
    e = pl.program_id(1)

    @pl.when(e == 0)
    def _():
        o_ref[...] = jnp.zeros_like(o_ref)

    et = u_ref.shape[0]
    heads = s_ref.shape[0]
    act = _gelu(lax.dot_general(u_ref[...], h_ref[...], _NT, preferred_element_type=F32))
    parts = []
    for ii in range(et // nk):
        i = e * (et // nk) + ii
        wsum = jnp.zeros((nk, act.shape[1]), F32)
        for h in range(heads):
            c = s_ref[h, 0, pl.ds(i, 1), :] + s_ref[h, 1]
            wsum = wsum + jnp.where(c >= st_ref[h, 0:1, :], jnp.exp(c - st_ref[h, 1:2, :]), 0.0)
        parts.append((wsum * act[ii * nk:(ii + 1) * nk, :]).astype(BF16))
    wg = jnp.concatenate(parts, axis=0) if len(parts) > 1 else parts[0]
    o_ref[...] += lax.dot_general(wg, v_ref[...].astype(BF16), _TN, preferred_element_type=F32)


def _peer_mix(h2, u, v, scores, stats, tt, et):
    m, d = h2.shape
    ne = u.shape[0]
    heads, _, nk, _ = scores.shape
    return pl.pallas_call(
        functools.partial(_peer_mix_body, nk=nk),
        grid=(m // tt, ne // et),
        in_specs=[pl.BlockSpec((tt, d), lambda t, e: (t, 0)),
                  pl.BlockSpec((et, d), lambda t, e: (e, 0)),
                  pl.BlockSpec((et, d), lambda t, e: (e, 0)),
                  pl.BlockSpec((heads, 2, nk, tt), lambda t, e: (0, 0, 0, t)),
                  pl.BlockSpec((heads, PEER_STATS, tt), lambda t, e: (0, 0, t))],
        out_specs=pl.BlockSpec((tt, d), lambda t, e: (t, 0)),
        out_shape=jax.ShapeDtypeStruct((m, d), F32),
        compiler_params=_cparams("arbitrary", "arbitrary"),
        name="peer_mix",
    )(h2, u, v, scores, stats)


def _final_body(x_ref, p_ref, gate_ref, fg_ref, o_ref):
    xn = x_ref[0] + gate_ref[0] * p_ref[...]
    o_ref[0] = _mean_sq_norm(xn) * fg_ref[...]


def _final(x3, peer, row0, mod, k_gate, final_g):
    gg, r, d = x3.shape
    tm = _row_tile(r, 256)
    nb = r // tm
    r0 = row0 // tm
    return pl.pallas_call(
        _final_body,
        grid=(gg, nb),
        in_specs=[pl.BlockSpec((1, tm, d), lambda b, i: (b, i, 0)),
                  pl.BlockSpec((tm, d), lambda b, i: (r0 + b * nb + i, 0)),
                  _mod_spec(mod, k_gate, tm, d),
                  pl.BlockSpec((1, d), lambda b, i: (0, 0))],
        out_specs=pl.BlockSpec((1, tm, d), lambda b, i: (b, i, 0)),
        out_shape=jax.ShapeDtypeStruct((gg, r, d), F32),
        compiler_params=_cparams("arbitrary", "arbitrary"),
        name="final",
    )(x3, peer, mod, final_g)


def _mixer(x3, mod, w, dims, *, sample, ssm0=None, conv0=None):
    d, d_inner, groups, d_state, headdim, conv_dim = dims
    heads = d_inner // headdim
    hpg = heads // groups
    h = _norm_mod(x3, w["norm1_g"], mod, 0, 1)
    z = _matmul(h, w["w_z"], _ep_none, name="in_z")
    xbc = _matmul(h, w["w_xbc"], _ep_none, name="in_xbc")
    dt = _matmul(h, w["w_dt"], _ep_softplus_bias, extras=[(w["dt_bias"], "bias")], name="in_dt")
    guv = _matmul(h, w["w_uv"], _ep_gelu, name="in_uv")
    sg = _matmul(h, w["w_g"], _ep_sigmoid, name="in_gates")

    if not sample:
        b, ln, _ = x3.shape
        xbc_act = _conv_prompt(xbc, w["conv_w"], w["conv_b"])
        conv_new = xbc[:, ln - (w["conv_w"].shape[0] - 1):, :]
        dt_g = jnp.pad(dt.reshape(b, ln, groups, hpg), ((0, 0), (0, 0), (0, 0), (0, LANES - hpg)))
        dt_g = dt_g.reshape(b, ln, groups * LANES)
        yn, h_new = _ssd_prompt(xbc_act, dt_g, z, w["alog_g"], w["dsk_e"], w["ssd_norm_g"],
                                d_inner=d_inner, groups=groups, d_state=d_state, headdim=headdim)
        ssm_new = h_new.reshape(b, heads, headdim, d_state)
        yb = _gmlp_prompt(guv, w["gmlp_ln_g"], w["gmlp_ln_b"], w["gmlp_ws"], w["gmlp_bs_t"])
        v_rows = None
    else:
        b = conv0.shape[0]
        ln = x3.shape[1] // b
        cw = w["conv_w"].shape[0]

        def slabs(t):
            return t.reshape(ln, b, t.shape[-1])

        def flat(t):
            return t.reshape(1, ln * b, t.shape[-1])

        xp = jnp.concatenate([jnp.transpose(conv0, (1, 0, 2)), slabs(xbc)], axis=0)
        xbc_act = _conv_slab(xp, w["conv_w"], w["conv_b"])
        conv_new = jnp.transpose(xp[ln:ln + cw - 1], (1, 0, 2))

        def to_rows(t):
            return jnp.pad(jnp.transpose(t, (1, 0, 2)), ((0, 0), (0, SUBLANES - ln), (0, 0)))

        dte_b = jnp.repeat(to_rows(slabs(dt)), headdim, axis=2)
        yn_b, h_new = _ssd_sample(to_rows(xbc_act), dte_b, to_rows(slabs(z)), w["a_e"], w["dsk_e"],
                                  w["ssd_norm_g"], ssm0.reshape(b, d_inner, d_state), groups=groups, valid=ln)
        ssm_new = h_new.reshape(b, heads, headdim, d_state)
        yn = flat(jnp.transpose(yn_b[:, :ln], (1, 0, 2)).astype(BF16))
        wrow, brow = _slab_gate_rows(w, ln)
        yb, v_tm = _gmlp_slab(slabs(guv), w["gmlp_ln_g"], w["gmlp_ln_b"], wrow, brow)
        yb = flat(yb)
        v_rows = jnp.transpose(v_tm, (1, 0, 2))

    y_a = _matmul(yn, w["w_oa"], _ep_none, name="out_a")
    merged = _matmul(yb, w["w_ob"], _ep_merge, extras=[(y_a, "tile"), (sg, "tile", 0), (sg, "tile", 1)],
                     out_dtype=BF16, name="out_b_merge")
    x1 = _matmul(merged, w["w_out"], _ep_residual, extras=[(x3, "tile"), (mod, "mod", 2)], name="out_proj")
    h2 = _norm_mod(x1, w["norm2_g"], mod, 3, 4)
    return x1, h2, ssm_new, conv_new, v_rows


def _slab_gate_rows(w, ln):
    ws, bs = w["gmlp_ws"], w["gmlp_bs"]
    gw = w["gmlp_ln_g"].shape[1] // ws.shape[0]
    wt = jnp.where(jnp.tril(jnp.ones((ln, ln), bool)), ws[:, :ln, :ln], 0.0)
    wrow = jnp.repeat(jnp.transpose(wt, (1, 2, 0)).reshape(ln * ln, -1), gw, axis=1)
    brow = jnp.repeat(jnp.transpose(bs[:, :ln]), gw, axis=1)
    return wrow, brow


def kernel(x_prompt, x_sample, state_ssm, state_conv, c_prompt, c_sample, w_ada, b_ada, norm1_g, w_in, conv_w, conv_b, dt_bias, a_log, d_skip, ssd_norm_g, w_oa, gmlp_ln_g, gmlp_ln_b, gmlp_ws, gmlp_bs, w_ob, w_out, norm2_g, peer_wq, peer_keys, peer_u, peer_v, final_g):
    depth = w_ada.shape[0]
    bp, lp, d = x_prompt.shape
    bs_, ls, _ = x_sample.shape
    heads = a_log.shape[1]
    headdim, d_state = state_ssm.shape[3], state_ssm.shape[4]
    d_inner = heads * headdim
    conv_dim = conv_w.shape[2]
    groups = (conv_dim - d_inner) // (2 * d_state)
    hpg = heads // groups
    d_gmlp = gmlp_ln_g.shape[1]
    dims = (d, d_inner, groups, d_state, headdim, conv_dim)
    off_xbc = d_inner
    off_dt = off_xbc + conv_dim
    off_u = off_dt + heads
    off_ga = off_u + 2 * d_gmlp

    xp = x_prompt
    xs = jnp.transpose(x_sample, (1, 0, 2)).reshape(1, ls * bs_, d)
    n_c = bs_ + bp
    c_all = jnp.pad(jnp.concatenate([c_sample, c_prompt], axis=0), ((0, (-n_c) % SUBLANES), (0, 0)))
    ssm_p, conv_p, ssm_s, conv_s, v_s = [], [], [], [], []
    for l in range(depth):
        mod = _ada(c_all, w_ada[l], b_ada[l][None, :])
        mod_s = jnp.tile(mod[:bs_], (ls, 1))[None]
        mod_p = mod[bs_:n_c][:, None, :]
        wl = w_in[l].astype(BF16)

        def family(start, n):
            return (wl, (start, n)) if start % LANES == 0 and n % LANES == 0 else wl[:, start:start + n]

        a = -jnp.exp(a_log[l])
        w = dict(
            norm1_g=norm1_g[l][None, :], norm2_g=norm2_g[l][None, :],
            w_z=family(0, d_inner), w_xbc=family(off_xbc, conv_dim), w_dt=family(off_dt, heads),
            w_uv=wl[:, off_u:off_ga], w_g=wl[:, off_ga:], dt_bias=dt_bias[l][None, :],
            conv_w=conv_w[l], conv_b=conv_b[l][None, :],
            alog_g=jnp.pad(a_log[l].reshape(groups, hpg), ((0, 0), (0, LANES - hpg))).reshape(1, groups * LANES),
            a_e=jnp.repeat(a, headdim)[None, :], dsk_e=jnp.repeat(d_skip[l], headdim)[None, :],
            ssd_norm_g=ssd_norm_g[l][None, :],
            w_oa=w_oa[l].astype(BF16), w_ob=w_ob[l].astype(BF16), w_out=w_out[l].astype(BF16),
            gmlp_ln_g=gmlp_ln_g[l][None, :], gmlp_ln_b=gmlp_ln_b[l][None, :],
            gmlp_ws=gmlp_ws[l], gmlp_bs=gmlp_bs[l], gmlp_bs_t=jnp.transpose(gmlp_bs[l]),
        )
        x1p, h2p, hp, cp, _ = _mixer(xp, mod_p, w, dims, sample=False)
        x1s, h2s, hs, cs, vs = _mixer(xs, mod_s, w, dims, sample=True, ssm0=state_ssm[l], conv0=state_conv[l])

        h2 = jnp.concatenate([h2p.reshape(bp * lp, d), h2s.reshape(ls * bs_, d)], axis=0)
        m = h2.shape[0]
        tt = _row_tile(m, 512)
        ne = peer_u.shape[1]
        et = _row_tile(ne, 256)
        scores = _peer_scores(h2, peer_wq[l].astype(BF16), peer_keys[l].astype(BF16), tt)
        stats = _peer_stats(scores, tt)
        peer_t = _peer_mix(h2, peer_u[l].astype(BF16), peer_v[l], scores, stats, tt, et)
        last = l == depth - 1
        fg = final_g[None, :] if last else jnp.ones((1, d), F32)
        assert last, "only DEPTH == 1 is supported"
        xp = _final(x1p, peer_t, 0, mod_p, 5, fg)
        xs = _final(x1s, peer_t, bp * lp, mod_s, 5, fg)
        ssm_p.append(hp)
        conv_p.append(cp)
        ssm_s.append(hs)
        conv_s.append(cs)
        v_s.append(vs)
    y_prompt = xp
    y_sample = jnp.transpose(xs.reshape(ls, bs_, d), (1, 0, 2))
    return (y_prompt, y_sample, jnp.stack(ssm_p), jnp.stack(conv_p), jnp.stack(ssm_s), jnp.stack(conv_s),
            jnp.stack(v_s))
```
